```python
import jax, jax.numpy as jnp
from jax import lax
import numpy as np

D_MODEL = 2048
BATCH = 4
SEQ = 4096
DEPTH = 4

CHUNK = 64
EPS = 1e-6
D_PLE = 256
D_FF = 5632
SGU_BLOCK = 128
D_A = 1024
N_GROUPS_A = 8
DG_A = D_A // N_GROUPS_A
D_B = 512
CONV_W = 3
D_C = 512
POOL_WINDOWS = (2, 4, 8, 16)
N_GROUPS_C = len(POOL_WINDOWS)
DG_C = D_C // N_GROUPS_C
N_BRANCH = 3
D_MIX = D_A + D_B + D_C
N_IN = 2 * D_A + 3 * D_B + D_C + N_BRANCH * D_MODEL

kernel_name = "hybrid_gmlp_shortconv_pool_macaron_trunk"


def _rmsnorm(x, g):
    xf = x.astype(jnp.float32)
    y = xf * lax.rsqrt(jnp.mean(xf * xf, axis=-1, keepdims=True) + EPS)
    return y.astype(x.dtype) * g


def _layernorm(x, g, b):
    xf = x.astype(jnp.float32)
    mu = jnp.mean(xf, axis=-1, keepdims=True)
    var = jnp.mean(jnp.square(xf - mu), axis=-1, keepdims=True)
    y = (xf - mu) * lax.rsqrt(var + EPS)
    return y.astype(x.dtype) * g + b


def _swiglu(x, w_gate, w_up, w_down):
    return (jax.nn.silu(x @ w_gate) * (x @ w_up)) @ w_down


def _sgu_mixer(z, sgu_norm_g, sgu_norm_b, sgu_w, sgu_b):
    bsz, seq, _ = z.shape
    u, v = z[..., :D_A], z[..., D_A:]
    v = _layernorm(v, sgu_norm_g, sgu_norm_b)
    v = v.reshape(bsz, seq // SGU_BLOCK, SGU_BLOCK, N_GROUPS_A, DG_A)
    pos = jnp.arange(SGU_BLOCK)
    mask = (pos[None, :] // CHUNK) <= (pos[:, None] // CHUNK)
    w = jnp.where(mask[None], sgu_w, jnp.zeros_like(sgu_w))
    vm = jnp.einsum('gts,bnsgc->bntgc', w, v) + jnp.transpose(sgu_b)[:, :, None]
    return u * vm.reshape(bsz, seq, D_A)


def _short_conv_mixer(zb, conv_w):
    b_gate, c_gate, xin = zb[..., :D_B], zb[..., D_B:2 * D_B], zb[..., 2 * D_B:]
    y = c_gate * xin
    y = lax.conv_general_dilated(
        y, conv_w[:, None, :].astype(y.dtype), window_strides=(1,), padding=[(CONV_W - 1, 0)],
        dimension_numbers=('NWC', 'WIO', 'NWC'), feature_group_count=D_B)
    return b_gate * y


def _pool_mixer(xc, pool_w, pool_scale):
    bsz, seq, _ = xc.shape
    xf = xc.astype(jnp.float32)
    cs = jnp.pad(jnp.cumsum(xf, axis=1), ((0, 0), (1, 0), (0, 0)))
    t1 = jnp.arange(1, seq + 1, dtype=jnp.float32)
    outs = []
    for g, w in enumerate(POOL_WINDOWS):
        csg = cs[:, :, g * DG_C:(g + 1) * DG_C]
        hi = csg[:, 1:]
        lo = jnp.pad(csg[:, :seq + 1 - w], ((0, 0), (w - 1, 0), (0, 0)))
        cnt = jnp.minimum(t1, jnp.float32(w))[None, :, None]
        outs.append((hi - lo) / cnt)
    pooled = (jnp.concatenate(outs, axis=-1) - xf).astype(xc.dtype)
    pooled = pooled.reshape(bsz, seq, N_GROUPS_C, DG_C)
    y = jnp.einsum('bsgc,gcd->bsgd', pooled, pool_w).reshape(bsz, seq, D_C)
    return y * pool_scale


def _layer(x, p_i, ffn1_norm, ffn1_w_gate, ffn1_w_up, ffn1_w_down, mix_norm, w_in, sgu_norm_g, sgu_norm_b,
           sgu_w, sgu_b, conv_w, pool_w, pool_scale, w_branch_a, w_branch_b, w_branch_c, w_out,
           ffn2_norm, ffn2_w_gate, ffn2_w_up, ffn2_w_down, ple_norm, ple_w_gate, ple_w_proj):
    bsz, seq, _ = x.shape
    h = x + 0.5 * _swiglu(_rmsnorm(x, ffn1_norm), ffn1_w_gate, ffn1_w_up, ffn1_w_down)
    n = _rmsnorm(h, mix_norm)
    z = n @ w_in
    o1 = 2 * D_A
    o2 = o1 + 3 * D_B
    o3 = o2 + D_C
    ya = _sgu_mixer(jax.nn.gelu(z[..., :o1], approximate=False), sgu_norm_g, sgu_norm_b, sgu_w, sgu_b)
    yb = _short_conv_mixer(z[..., o1:o2], conv_w)
    yc = _pool_mixer(z[..., o2:o3], pool_w, pool_scale)
    gates = jax.nn.sigmoid(z[..., o3:]).reshape(bsz, seq, N_BRANCH, D_MODEL)
    m = (gates[:, :, 0] * (ya @ w_branch_a)
         + gates[:, :, 1] * (yb @ w_branch_b)
         + gates[:, :, 2] * (yc @ w_branch_c))
    h = h + m @ w_out
    h = h + 0.5 * _swiglu(_rmsnorm(h, ffn2_norm), ffn2_w_gate, ffn2_w_up, ffn2_w_down)
    h = h + jax.nn.sigmoid(_rmsnorm(h, ple_norm) @ ple_w_gate) * (p_i @ ple_w_proj)
    return h


def setup_inputs(seed: int = 0) -> dict:
    key = jax.random.key(seed)
    ks = iter(jax.random.split(key, 40))
    L, D, F = DEPTH, D_MODEL, D_FF

    def nrm(shape, scale):
        return jax.random.normal(next(ks), shape, jnp.float32) * scale

    def gain(shape):
        return 1.0 + nrm(shape, 0.01)

    return {
        "x": nrm((BATCH, SEQ, D), 1.0),
        "p": nrm((L, BATCH, SEQ, D_PLE), 1.0),
        "ffn1_norm": gain((L, D)),
        "ffn1_w_gate": nrm((L, D, F), D ** -0.5),
        "ffn1_w_up": nrm((L, D, F), D ** -0.5),
        "ffn1_w_down": nrm((L, F, D), F ** -0.5),
        "mix_norm": gain((L, D)),
        "w_in": nrm((L, D, N_IN), D ** -0.5),
        "sgu_norm_g": gain((L, D_A)),
        "sgu_norm_b": nrm((L, D_A), 0.01),
        "sgu_w": nrm((L, N_GROUPS_A, SGU_BLOCK, SGU_BLOCK), SGU_BLOCK ** -0.5),
        "sgu_b": gain((L, N_GROUPS_A, SGU_BLOCK)),
        "conv_w": nrm((L, CONV_W, D_B), CONV_W ** -0.5),
        "pool_w": nrm((L, N_GROUPS_C, DG_C, DG_C), DG_C ** -0.5),
        "pool_scale": gain((L, D_C)),
        "w_branch_a": nrm((L, D_A, D), D_A ** -0.5),
        "w_branch_b": nrm((L, D_B, D), D_B ** -0.5),
        "w_branch_c": nrm((L, D_C, D), D_C ** -0.5),
        "w_out": nrm((L, D, D), D ** -0.5),
        "ffn2_norm": gain((L, D)),
        "ffn2_w_gate": nrm((L, D, F), D ** -0.5),
        "ffn2_w_up": nrm((L, D, F), D ** -0.5),
        "ffn2_w_down": nrm((L, F, D), F ** -0.5),
        "ple_norm": gain((L, D)),
        "ple_w_gate": nrm((L, D, D), D ** -0.5),
        "ple_w_proj": nrm((L, D_PLE, D), D_PLE ** -0.5),
        "final_norm": gain((D,)),
    }


def reference(x, p, ffn1_norm, ffn1_w_gate, ffn1_w_up, ffn1_w_down, mix_norm, w_in, sgu_norm_g, sgu_norm_b,
              sgu_w, sgu_b, conv_w, pool_w, pool_scale, w_branch_a, w_branch_b, w_branch_c, w_out,
              ffn2_norm, ffn2_w_gate, ffn2_w_up, ffn2_w_down, ple_norm, ple_w_gate, ple_w_proj, final_norm):
    h = x
    for i in range(DEPTH):
        h = _layer(h, p[i], ffn1_norm[i], ffn1_w_gate[i], ffn1_w_up[i], ffn1_w_down[i], mix_norm[i], w_in[i],
                   sgu_norm_g[i], sgu_norm_b[i], sgu_w[i], sgu_b[i], conv_w[i], pool_w[i], pool_scale[i],
                   w_branch_a[i], w_branch_b[i], w_branch_c[i], w_out[i], ffn2_norm[i], ffn2_w_gate[i],
                   ffn2_w_up[i], ffn2_w_down[i], ple_norm[i], ple_w_gate[i], ple_w_proj[i])
    return _rmsnorm(h, final_norm)
```

```python
import functools

import jax
import jax.numpy as jnp
from jax import lax
from jax.experimental import pallas as pl
from jax.experimental.pallas import tpu as pltpu

EPS = 1e-6
CHUNK = 64
SGU_BLOCK = 128
CONV_W = 3
POOL_WINDOWS = (2, 4, 8, 16)
HALO = 16

V7X_VMEM_BYTES = 64 * 1024 * 1024
VMEM_LIMIT_BYTES = V7X_VMEM_BYTES - 6 * 1024 * 1024

F32 = jnp.float32
BF16 = jnp.bfloat16


def _pick(pref, n):
    t = min(pref, n)
    assert n % t == 0, (pref, n)
    return t


def _rmsnorm(x, g):
    return x * lax.rsqrt(jnp.mean(x * x, axis=-1, keepdims=True) + EPS) * g


def _gelu(x):
    return 0.5 * x * (1.0 + lax.erf(x * (2.0 ** -0.5)))


def _for_rows(total, chunk, body):
    n = total // chunk
    if n == 1:
        body(pl.ds(0, chunk))
        return

    def step(r, carry):
        body(pl.ds(pl.multiple_of(r * chunk, chunk), chunk))
        return carry

    lax.fori_loop(0, n, step, 0)


def _ffn_kernel(x_ref, g_ref, wg_ref, wu_ref, wd_ref, o_ref, xn_ref, *, row_chunk):
    tm = x_ref.shape[0]

    @pl.when(pl.program_id(1) == 0)
    def _():
        def init(rows):
            x = x_ref[rows, :]
            xn_ref[rows, :] = _rmsnorm(x, g_ref[...]).astype(BF16)
            o_ref[rows, :] = x
        _for_rows(tm, row_chunk, init)

    def step(rows):
        xn = xn_ref[rows, :]
        gate = jnp.dot(xn, wg_ref[...], preferred_element_type=F32)
        up = jnp.dot(xn, wu_ref[...], preferred_element_type=F32)
        act = (0.5 * jax.nn.silu(gate) * up).astype(BF16)
        o_ref[rows, :] += jnp.dot(act, wd_ref[...], preferred_element_type=F32)
    _for_rows(tm, row_chunk, step)


def _ffn(x, norm_g, w_gate, w_up, w_down, layer, *, tm=1024, tf=512, row_chunk=256):
    t, d = x.shape
    f = w_gate.shape[-1]
    tm = _pick(tm, t)
    tf = _pick(tf, f)
    row_chunk = _pick(row_chunk, tm)
    return pl.pallas_call(
        functools.partial(_ffn_kernel, row_chunk=row_chunk),
        grid=(t // tm, f // tf),
        in_specs=[
            pl.BlockSpec((tm, d), lambda i, j: (i, 0)),
            pl.BlockSpec((None, 1, d), lambda i, j: (layer, 0, 0)),
            pl.BlockSpec((None, d, tf), lambda i, j: (layer, 0, j)),
            pl.BlockSpec((None, d, tf), lambda i, j: (layer, 0, j)),
            pl.BlockSpec((None, tf, d), lambda i, j: (layer, j, 0)),
        ],
        out_specs=pl.BlockSpec((tm, d), lambda i, j: (i, 0)),
        out_shape=jax.ShapeDtypeStruct((t, d), F32),
        scratch_shapes=[pltpu.VMEM((tm, d), BF16)],
        compiler_params=pltpu.CompilerParams(
            dimension_semantics=("parallel", "arbitrary"), vmem_limit_bytes=VMEM_LIMIT_BYTES),
        name="ffn",
    )(x, norm_g, w_gate, w_up, w_down)


def _inproj_kernel(h_ref, g_ref, w_ref, z_ref, xn_ref, *, row_chunk, gelu_tiles, linear_tiles):
    tm = h_ref.shape[0]
    j = pl.program_id(1)

    @pl.when(j == 0)
    def _():
        def init(rows):
            xn_ref[rows, :] = _rmsnorm(h_ref[rows, :], g_ref[...]).astype(BF16)
        _for_rows(tm, row_chunk, init)

    def project(act):
        def step(rows):
            z = jnp.dot(xn_ref[rows, :], w_ref[...], preferred_element_type=F32)
            z_ref[rows, :] = act(z).astype(z_ref.dtype)
        _for_rows(tm, row_chunk, step)

    @pl.when(j < gelu_tiles)
    def _():
        project(_gelu)

    @pl.when((j >= gelu_tiles) & (j < gelu_tiles + linear_tiles))
    def _():
        project(lambda z: z)

    @pl.when(j >= gelu_tiles + linear_tiles)
    def _():
        project(jax.nn.sigmoid)


def _inproj(h, norm_g, w_in, layer, n_gelu, n_linear, *, tm=1024, tn=512, row_chunk=256):
    t, d = h.shape
    n = w_in.shape[-1]
    tm = _pick(tm, t)
    tn = _pick(tn, n)
    row_chunk = _pick(row_chunk, tm)
    assert n_gelu % tn == 0 and n_linear % tn == 0
    return pl.pallas_call(
        functools.partial(_inproj_kernel, row_chunk=row_chunk, gelu_tiles=n_gelu // tn,
                          linear_tiles=n_linear // tn),
        grid=(t // tm, n // tn),
        in_specs=[
            pl.BlockSpec((tm, d), lambda i, j: (i, 0)),
            pl.BlockSpec((None, 1, d), lambda i, j: (layer, 0, 0)),
            pl.BlockSpec((None, d, tn), lambda i, j: (layer, 0, j)),
        ],
        out_specs=pl.BlockSpec((tm, tn), lambda i, j: (i, j)),
        out_shape=jax.ShapeDtypeStruct((t, n), BF16),
        scratch_shapes=[pltpu.VMEM((tm, d), BF16)],
        compiler_params=pltpu.CompilerParams(
            dimension_semantics=("parallel", "arbitrary"), vmem_limit_bytes=VMEM_LIMIT_BYTES),
        name="inproj",
    )(h, norm_g, w_in)


def _mixer_kernel(z_ref, h_ref, lng_ref, lnb_ref, sguw_ref, sgub_ref, convw_ref, poolw_ref, pscale_ref,
                  wa_ref, wb_ref, wc_ref, wout_ref, o_ref, ybuf_ref, xbuf_ref, *, tiles_per_seq, d_a, d_b, d_c):
    tm, d = h_ref.shape
    n_groups_a = sguw_ref.shape[0]
    dg_a = d_a // n_groups_a
    dg_c = d_c // len(POOL_WINDOWS)
    o1 = 2 * d_a
    o2 = o1 + 3 * d_b
    o3 = o2 + d_c
    tile_in_seq = pl.program_id(0) % tiles_per_seq

    @pl.when(tile_in_seq == 0)
    def _():
        ybuf_ref[0:HALO, :] = jnp.zeros((HALO, d_b), F32)
        xbuf_ref[0:HALO, :] = jnp.zeros((HALO, d_c), F32)

    v = z_ref[:, d_a:o1].astype(F32)
    mu = jnp.mean(v, axis=-1, keepdims=True)
    vc = v - mu
    var = jnp.mean(vc * vc, axis=-1, keepdims=True)
    vn = (vc * lax.rsqrt(var + EPS) * lng_ref[...] + lnb_ref[...]).astype(BF16)
    blocks = []
    for blk in range(tm // SGU_BLOCK):
        rows = slice(blk * SGU_BLOCK, (blk + 1) * SGU_BLOCK)
        groups = [jnp.dot(sguw_ref[g], vn[rows, g * dg_a:(g + 1) * dg_a], preferred_element_type=F32)
                  for g in range(n_groups_a)]
        blocks.append(jnp.concatenate(groups, axis=1) + sgub_ref[...])
    vm = jnp.concatenate(blocks, axis=0) if len(blocks) > 1 else blocks[0]
    ya = (z_ref[:, 0:d_a].astype(F32) * vm).astype(BF16)

    ybuf_ref[HALO:HALO + tm, :] = (z_ref[:, o1 + d_b:o1 + 2 * d_b].astype(F32)
                                   * z_ref[:, o1 + 2 * d_b:o2].astype(F32))
    conv = convw_ref[CONV_W - 1:CONV_W, :] * ybuf_ref[HALO:HALO + tm, :]
    for k in range(CONV_W - 1):
        back = CONV_W - 1 - k
        conv += convw_ref[k:k + 1, :] * ybuf_ref[HALO - back:HALO - back + tm, :]
    yb = (z_ref[:, o1:o1 + d_b].astype(F32) * conv).astype(BF16)
    ybuf_ref[0:HALO, :] = ybuf_ref[tm:tm + HALO, :]

    xbuf_ref[HALO:HALO + tm, :] = z_ref[:, o2:o3].astype(F32)
    frames = (tile_in_seq * tm + 1 + lax.broadcasted_iota(jnp.int32, (tm, 1), 0)).astype(F32)
    yc_groups = []
    for g, w in enumerate(POOL_WINDOWS):
        cols = slice(g * dg_c, (g + 1) * dg_c)
        cur = xbuf_ref[HALO:HALO + tm, cols]
        total = cur
        for back in range(1, w):
            total += xbuf_ref[HALO - back:HALO - back + tm, cols]
        pooled = (total / jnp.minimum(frames, F32(w)) - cur).astype(BF16)
        yc_groups.append(jnp.dot(pooled, poolw_ref[g], preferred_element_type=F32))
    yc = (jnp.concatenate(yc_groups, axis=1) * pscale_ref[...]).astype(BF16)
    xbuf_ref[0:HALO, :] = xbuf_ref[tm:tm + HALO, :]

    m = z_ref[:, o3:o3 + d].astype(F32) * jnp.dot(ya, wa_ref[...], preferred_element_type=F32)
    m += z_ref[:, o3 + d:o3 + 2 * d].astype(F32) * jnp.dot(yb, wb_ref[...], preferred_element_type=F32)
    m += z_ref[:, o3 + 2 * d:o3 + 3 * d].astype(F32) * jnp.dot(yc, wc_ref[...], preferred_element_type=F32)
    o_ref[...] = h_ref[...] + jnp.dot(m.astype(BF16), wout_ref[...], preferred_element_type=F32)


def _resident(shape, layer):
    zeros = (0,) * len(shape)
    return pl.BlockSpec((None,) + tuple(shape), lambda i: (layer,) + zeros, pipeline_mode=pl.Buffered(1))


def _mixer(z, h, seq, ln_g, ln_b, sgu_w, sgu_bias, conv_w, pool_w, pool_scale, w_a, w_b, w_c, w_out, layer,
           *, tm=256):
    t, d = h.shape
    n_in = z.shape[-1]
    d_a, d_b, d_c = w_a.shape[1], w_b.shape[1], w_c.shape[1]
    tm = _pick(tm, seq)
    assert tm % SGU_BLOCK == 0 and tm >= HALO
    params = (ln_g, ln_b, sgu_w, sgu_bias, conv_w, pool_w, pool_scale, w_a, w_b, w_c, w_out)
    return pl.pallas_call(
        functools.partial(_mixer_kernel, tiles_per_seq=seq // tm, d_a=d_a, d_b=d_b, d_c=d_c),
        grid=(t // tm,),
        in_specs=[pl.BlockSpec((tm, n_in), lambda i: (i, 0)), pl.BlockSpec((tm, d), lambda i: (i, 0))]
                 + [_resident(a.shape[1:], layer) for a in params],
        out_specs=pl.BlockSpec((tm, d), lambda i: (i, 0)),
        out_shape=jax.ShapeDtypeStruct((t, d), F32),
        scratch_shapes=[pltpu.VMEM((HALO + tm, d_b), F32), pltpu.VMEM((HALO + tm, d_c), F32)],
        compiler_params=pltpu.CompilerParams(
            dimension_semantics=("arbitrary",), vmem_limit_bytes=VMEM_LIMIT_BYTES),
        name="mixer",
    )(z, h, *params)


def _ple_kernel(h_ref, p_ref, g_ref, wg_ref, wp_ref, fg_ref, o_ref, *, final_norm):
    h = h_ref[...]
    gate = jnp.dot(_rmsnorm(h, g_ref[...]).astype(BF16), wg_ref[...], preferred_element_type=F32)
    proj = jnp.dot(p_ref[...].astype(BF16), wp_ref[...], preferred_element_type=F32)
    out = h + jax.nn.sigmoid(gate) * proj
    if final_norm:
        out = _rmsnorm(out, fg_ref[...])
    o_ref[...] = out


def _ple(h, p, norm_g, w_gate, w_proj, final_g, layer, final_norm, *, tm=256):
    t, d = h.shape
    d_ple = p.shape[-1]
    tm = _pick(tm, t)
    tiles = t // tm
    return pl.pallas_call(
        functools.partial(_ple_kernel, final_norm=final_norm),
        grid=(tiles,),
        in_specs=[
            pl.BlockSpec((tm, d), lambda i: (i, 0)),
            pl.BlockSpec((tm, d_ple), lambda i: (layer * tiles + i, 0)),
            _resident((1, d), layer),
            _resident((d, d), layer),
            _resident((d_ple, d), layer),
            pl.BlockSpec((1, d), lambda i: (0, 0)),
        ],
        out_specs=pl.BlockSpec((tm, d), lambda i: (i, 0)),
        out_shape=jax.ShapeDtypeStruct((t, d), F32),
        compiler_params=pltpu.CompilerParams(
            dimension_semantics=("parallel",), vmem_limit_bytes=VMEM_LIMIT_BYTES),
        name="ple",
    )(h, p, norm_g, w_gate, w_proj, final_g)


def kernel(x, p, ffn1_norm, ffn1_w_gate, ffn1_w_up, ffn1_w_down, mix_norm, w_in, sgu_norm_g, sgu_norm_b, sgu_w, sgu_b, conv_w, pool_w, pool_scale, w_branch_a, w_branch_b, w_branch_c, w_out, ffn2_norm, ffn2_w_gate, ffn2_w_up, ffn2_w_down, ple_norm, ple_w_gate, ple_w_proj, final_norm):
    bsz, seq, d = x.shape
    depth = p.shape[0]
    t = bsz * seq
    d_a = w_branch_a.shape[1]
    d_b = w_branch_b.shape[1]
    d_c = w_branch_c.shape[1]
    n_groups_a = sgu_w.shape[1]

    bf = lambda a: a.astype(BF16)
    row = lambda a: a[:, None, :]

    pos = jnp.arange(SGU_BLOCK)
    mask = (pos[None, :] // CHUNK) <= (pos[:, None] // CHUNK)
    sgu_w_masked = bf(jnp.where(mask[None, None], sgu_w, jnp.zeros_like(sgu_w)))
    sgu_bias = jnp.repeat(jnp.swapaxes(sgu_b, 1, 2), d_a // n_groups_a, axis=2)

    ffn1 = (row(ffn1_norm), bf(ffn1_w_gate), bf(ffn1_w_up), bf(ffn1_w_down))
    ffn2 = (row(ffn2_norm), bf(ffn2_w_gate), bf(ffn2_w_up), bf(ffn2_w_down))
    mix_g, w_in_b = row(mix_norm), bf(w_in)
    mixer_params = (row(sgu_norm_g), row(sgu_norm_b), sgu_w_masked, sgu_bias, conv_w, bf(pool_w),
                    row(pool_scale), bf(w_branch_a), bf(w_branch_b), bf(w_branch_c), bf(w_out))
    ple_g, ple_wg, ple_wp = row(ple_norm), bf(ple_w_gate), bf(ple_w_proj)
    final_g = final_norm[None, :]

    h = x.reshape(t, d)
    p2 = p.reshape(depth * t, p.shape[-1])
    for layer in range(depth):
        h = _ffn(h, *ffn1, layer)
        z = _inproj(h, mix_g, w_in_b, layer, 2 * d_a, 3 * d_b + d_c)
        h = _mixer(z, h, seq, *mixer_params, layer)
        h = _ffn(h, *ffn2, layer)
        h = _ple(h, p2, ple_g, ple_wg, ple_wp, final_g, layer, layer == depth - 1)
    return h.reshape(bsz, seq, d)
```

```python
import functools

import jax
import jax.numpy as jnp
from jax import lax
from jax.experimental import pallas as pl
from jax.experimental.pallas import tpu as pltpu

EPS = 1e-6
CHUNK = 64
SGU_BLOCK = 128
CONV_W = 3
POOL_WINDOWS = (2, 4, 8, 16)
HALO = 16

V7X_VMEM_BYTES = 64 * 1024 * 1024
VMEM_LIMIT_BYTES = V7X_VMEM_BYTES - 6 * 1024 * 1024

F32 = jnp.float32
BF16 = jnp.bfloat16


def _pick(pref, n):
    t = min(pref, n)
    assert n % t == 0, (pref, n)
    return t


def _rmsnorm(x, g):
    return x * lax.rsqrt(jnp.mean(x * x, axis=-1, keepdims=True) + EPS) * g


def _gelu(x):
    return 0.5 * x * (1.0 + lax.erf(x * (2.0 ** -0.5)))


def _for_rows(total, chunk, body, unroll=1):
    n = total // chunk
    if n == unroll:
        for r in range(n):
            body(pl.ds(r * chunk, chunk))
        return

    def step(r, carry):
        body(pl.ds(pl.multiple_of(r * chunk, chunk), chunk))
        return carry

    lax.fori_loop(0, n, step, 0, unroll=unroll)


def _ffn_kernel(x_ref, g_ref, wg_ref, wu_ref, wd_ref, o_ref, xn_ref, *, row_chunk, unroll):
    tm = x_ref.shape[0]

    @pl.when(pl.program_id(1) == 0)
    def _():
        def init(rows):
            x = x_ref[rows, :]
            xn_ref[rows, :] = _rmsnorm(x, g_ref[...]).astype(BF16)
            o_ref[rows, :] = x
        _for_rows(tm, min(row_chunk, 256), init)

    def step(rows):
        xn = xn_ref[rows, :]
        gate = jnp.dot(xn, wg_ref[...], preferred_element_type=F32)
        up = jnp.dot(xn, wu_ref[...], preferred_element_type=F32)
        act = (0.5 * jax.nn.silu(gate) * up).astype(BF16)
        o_ref[rows, :] += jnp.dot(act, wd_ref[...], preferred_element_type=F32)
    _for_rows(tm, row_chunk, step, unroll=unroll)


def _ffn(x, norm_g, w_gate, w_up, w_down, layer, *, name, tm=1024, tf=512, row_chunk=256, unroll=1, tiled=False):
    t, d = x.shape
    f = w_down.shape[1]
    tm = _pick(tm, t)
    tf = _pick(tf, f)
    row_chunk = _pick(row_chunk, tm)
    return pl.pallas_call(
        functools.partial(_ffn_kernel, row_chunk=row_chunk, unroll=unroll),
        grid=(t // tm, f // tf),
        in_specs=[
            pl.BlockSpec((tm, d), lambda i, j: (i, 0)),
            pl.BlockSpec((None, 1, d), lambda i, j: (layer, 0, 0)),
            (pl.BlockSpec((None, None, d, tf), lambda i, j: (layer, j, 0, 0)) if tiled
             else pl.BlockSpec((None, d, tf), lambda i, j: (layer, 0, j))),
            (pl.BlockSpec((None, None, d, tf), lambda i, j: (layer, j, 0, 0)) if tiled
             else pl.BlockSpec((None, d, tf), lambda i, j: (layer, 0, j))),
            pl.BlockSpec((None, tf, d), lambda i, j: (layer, j, 0)),
        ],
        out_specs=pl.BlockSpec((tm, d), lambda i, j: (i, 0)),
        out_shape=jax.ShapeDtypeStruct((t, d), F32),
        scratch_shapes=[pltpu.VMEM((tm, d), BF16)],
        compiler_params=pltpu.CompilerParams(
            dimension_semantics=("parallel", "arbitrary"), vmem_limit_bytes=VMEM_LIMIT_BYTES),
        name=name,
    )(x, norm_g, w_gate, w_up, w_down)


def _inproj_kernel(h_ref, g_ref, w_ref, z_ref, xn_ref, *, row_chunk, gelu_tiles, linear_tiles, unroll,
                   tanh_sigmoid):
    tm = h_ref.shape[0]
    j = pl.program_id(1)

    @pl.when(j == 0)
    def _():
        def init(rows):
            xn_ref[rows, :] = _rmsnorm(h_ref[rows, :], g_ref[...]).astype(BF16)
        _for_rows(tm, min(row_chunk, 256), init)

    def project(act):
        def step(rows):
            z = jnp.dot(xn_ref[rows, :], w_ref[...], preferred_element_type=F32)
            z_ref[rows, :] = act(z).astype(z_ref.dtype)
        _for_rows(tm, row_chunk, step, unroll=unroll)

    @pl.when(j < gelu_tiles)
    def _():
        project(_gelu)

    @pl.when((j >= gelu_tiles) & (j < gelu_tiles + linear_tiles))
    def _():
        project(lambda z: z)

    @pl.when(j >= gelu_tiles + linear_tiles)
    def _():
        project((lambda z: 0.5 + 0.5 * jnp.tanh(0.5 * z)) if tanh_sigmoid else jax.nn.sigmoid)


def _inproj(h, norm_g, w_in, layer, n_gelu, n_linear, *, name, tm=1024, tn=1024, row_chunk=512, unroll=2,
            tanh_sigmoid=False):
    t, d = h.shape
    n = w_in.shape[-1]
    tm = _pick(tm, t)
    tn = _pick(tn, n)
    row_chunk = _pick(row_chunk, tm)
    assert n_gelu % tn == 0 and n_linear % tn == 0
    return pl.pallas_call(
        functools.partial(_inproj_kernel, row_chunk=row_chunk, gelu_tiles=n_gelu // tn,
                          linear_tiles=n_linear // tn, unroll=unroll, tanh_sigmoid=tanh_sigmoid),
        grid=(t // tm, n // tn),
        in_specs=[
            pl.BlockSpec((tm, d), lambda i, j: (i, 0)),
            pl.BlockSpec((None, 1, d), lambda i, j: (layer, 0, 0)),
            pl.BlockSpec((None, d, tn), lambda i, j: (layer, 0, j)),
        ],
        out_specs=pl.BlockSpec((tm, tn), lambda i, j: (i, j)),
        out_shape=jax.ShapeDtypeStruct((t, n), BF16),
        scratch_shapes=[pltpu.VMEM((tm, d), BF16)],
        compiler_params=pltpu.CompilerParams(
            dimension_semantics=("parallel", "arbitrary"), vmem_limit_bytes=VMEM_LIMIT_BYTES),
        name=name,
    )(h, norm_g, w_in)


def _mixer_kernel(z_ref, h_ref, lng_ref, lnb_ref, sguw_ref, sgub_ref, convw_ref, poolw_ref, pscale_ref,
                  wa_ref, wb_ref, wc_ref, wout_ref, o_ref, ybuf_ref, xbuf_ref, *, tiles_per_seq, d_a, d_b, d_c):
    tm, d = h_ref.shape
    n_groups_a = sguw_ref.shape[0]
    dg_a = d_a // n_groups_a
    dg_c = d_c // len(POOL_WINDOWS)
    o1 = 2 * d_a
    o2 = o1 + 3 * d_b
    o3 = o2 + d_c
    tile_in_seq = pl.program_id(0) % tiles_per_seq

    @pl.when(tile_in_seq == 0)
    def _():
        ybuf_ref[0:HALO, :] = jnp.zeros((HALO, d_b), F32)
        xbuf_ref[0:HALO, :] = jnp.zeros((HALO, d_c), F32)

    v = z_ref[:, d_a:o1].astype(F32)
    mu = jnp.mean(v, axis=-1, keepdims=True)
    vc = v - mu
    var = jnp.mean(vc * vc, axis=-1, keepdims=True)
    vn = (vc * lax.rsqrt(var + EPS) * lng_ref[...] + lnb_ref[...]).astype(BF16)
    blocks = []
    for blk in range(tm // SGU_BLOCK):
        rows = slice(blk * SGU_BLOCK, (blk + 1) * SGU_BLOCK)
        groups = [jnp.dot(sguw_ref[g], vn[rows, g * dg_a:(g + 1) * dg_a], preferred_element_type=F32)
                  for g in range(n_groups_a)]
        blocks.append(jnp.concatenate(groups, axis=1) + sgub_ref[...])
    vm = jnp.concatenate(blocks, axis=0) if len(blocks) > 1 else blocks[0]
    ya = (z_ref[:, 0:d_a].astype(F32) * vm).astype(BF16)

    ybuf_ref[HALO:HALO + tm, :] = (z_ref[:, o1 + d_b:o1 + 2 * d_b].astype(F32)
                                   * z_ref[:, o1 + 2 * d_b:o2].astype(F32))
    conv = convw_ref[CONV_W - 1:CONV_W, :] * ybuf_ref[HALO:HALO + tm, :]
    for k in range(CONV_W - 1):
        back = CONV_W - 1 - k
        conv += convw_ref[k:k + 1, :] * ybuf_ref[HALO - back:HALO - back + tm, :]
    yb = (z_ref[:, o1:o1 + d_b].astype(F32) * conv).astype(BF16)
    ybuf_ref[0:HALO, :] = ybuf_ref[tm:tm + HALO, :]

    xbuf_ref[HALO:HALO + tm, :] = z_ref[:, o2:o3].astype(F32)
    frames = (tile_in_seq * tm + 1 + lax.broadcasted_iota(jnp.int32, (tm, 1), 0)).astype(F32)
    yc_groups = []
    for g, w in enumerate(POOL_WINDOWS):
        cols = slice(g * dg_c, (g + 1) * dg_c)
        cur = xbuf_ref[HALO:HALO + tm, cols]
        total = cur
        for back in range(1, w):
            total += xbuf_ref[HALO - back:HALO - back + tm, cols]
        pooled = (total / jnp.minimum(frames, F32(w)) - cur).astype(BF16)
        yc_groups.append(jnp.dot(pooled, poolw_ref[g], preferred_element_type=F32))
    yc = (jnp.concatenate(yc_groups, axis=1) * pscale_ref[...]).astype(BF16)
    xbuf_ref[0:HALO, :] = xbuf_ref[tm:tm + HALO, :]

    m = z_ref[:, o3:o3 + d].astype(F32) * jnp.dot(ya, wa_ref[...], preferred_element_type=F32)
    m += z_ref[:, o3 + d:o3 + 2 * d].astype(F32) * jnp.dot(yb, wb_ref[...], preferred_element_type=F32)
    m += z_ref[:, o3 + 2 * d:o3 + 3 * d].astype(F32) * jnp.dot(yc, wc_ref[...], preferred_element_type=F32)
    o_ref[...] = h_ref[...] + jnp.dot(m.astype(BF16), wout_ref[...], preferred_element_type=F32)


def _resident(shape, layer):
    zeros = (0,) * len(shape)
    return pl.BlockSpec((None,) + tuple(shape), lambda i: (layer,) + zeros, pipeline_mode=pl.Buffered(1))


def _mixer(z, h, seq, ln_g, ln_b, sgu_w, sgu_bias, conv_w, pool_w, pool_scale, w_a, w_b, w_c, w_out, layer,
           *, name, tm=256):
    t, d = h.shape
    n_in = z.shape[-1]
    d_a, d_b, d_c = w_a.shape[1], w_b.shape[1], w_c.shape[1]
    tm = _pick(tm, seq)
    assert tm % SGU_BLOCK == 0 and tm >= HALO
    params = (ln_g, ln_b, sgu_w, sgu_bias, conv_w, pool_w, pool_scale, w_a, w_b, w_c, w_out)
    return pl.pallas_call(
        functools.partial(_mixer_kernel, tiles_per_seq=seq // tm, d_a=d_a, d_b=d_b, d_c=d_c),
        grid=(t // tm,),
        in_specs=[pl.BlockSpec((tm, n_in), lambda i: (i, 0)), pl.BlockSpec((tm, d), lambda i: (i, 0))]
                 + [_resident(a.shape[1:], layer) for a in params],
        out_specs=pl.BlockSpec((tm, d), lambda i: (i, 0)),
        out_shape=jax.ShapeDtypeStruct((t, d), F32),
        scratch_shapes=[pltpu.VMEM((HALO + tm, d_b), F32), pltpu.VMEM((HALO + tm, d_c), F32)],
        compiler_params=pltpu.CompilerParams(
            dimension_semantics=("arbitrary",), vmem_limit_bytes=VMEM_LIMIT_BYTES),
        name=name,
    )(z, h, *params)


def _ple_kernel(h_ref, p_ref, g_ref, wg_ref, wp_ref, fg_ref, o_ref, *, final_norm):
    h = h_ref[...]
    gate = jnp.dot(_rmsnorm(h, g_ref[...]).astype(BF16), wg_ref[...], preferred_element_type=F32)
    proj = jnp.dot(p_ref[...].astype(BF16), wp_ref[...], preferred_element_type=F32)
    out = h + jax.nn.sigmoid(gate) * proj
    if final_norm:
        out = _rmsnorm(out, fg_ref[...])
    o_ref[...] = out


def _ple(h, p, norm_g, w_gate, w_proj, final_g, layer, final_norm, *, name, tm=256):
    t, d = h.shape
    d_ple = p.shape[-1]
    tm = _pick(tm, t)
    tiles = t // tm
    return pl.pallas_call(
        functools.partial(_ple_kernel, final_norm=final_norm),
        grid=(tiles,),
        in_specs=[
            pl.BlockSpec((tm, d), lambda i: (i, 0)),
            pl.BlockSpec((tm, d_ple), lambda i: (layer * tiles + i, 0)),
            _resident((1, d), layer),
            _resident((d, d), layer),
            _resident((d_ple, d), layer),
            pl.BlockSpec((1, d), lambda i: (0, 0)),
        ],
        out_specs=pl.BlockSpec((tm, d), lambda i: (i, 0)),
        out_shape=jax.ShapeDtypeStruct((t, d), F32),
        compiler_params=pltpu.CompilerParams(
            dimension_semantics=("parallel",), vmem_limit_bytes=VMEM_LIMIT_BYTES),
        name=name,
    )(h, p, norm_g, w_gate, w_proj, final_g)


def kernel(x, p, ffn1_norm, ffn1_w_gate, ffn1_w_up, ffn1_w_down, mix_norm, w_in, sgu_norm_g, sgu_norm_b, sgu_w, sgu_b, conv_w, pool_w, pool_scale, w_branch_a, w_branch_b, w_branch_c, w_out, ffn2_norm, ffn2_w_gate, ffn2_w_up, ffn2_w_down, ple_norm, ple_w_gate, ple_w_proj, final_norm):
    bsz, seq, d = x.shape
    depth = p.shape[0]
    t = bsz * seq
    d_a = w_branch_a.shape[1]
    d_b = w_branch_b.shape[1]
    d_c = w_branch_c.shape[1]
    n_groups_a = sgu_w.shape[1]

    bf = lambda a: a.astype(BF16)
    row = lambda a: a[:, None, :]

    pos = jnp.arange(SGU_BLOCK)
    mask = (pos[None, :] // CHUNK) <= (pos[:, None] // CHUNK)
    sgu_w_masked = bf(jnp.where(mask[None, None], sgu_w, jnp.zeros_like(sgu_w)))
    sgu_bias = jnp.repeat(jnp.swapaxes(sgu_b, 1, 2), d_a // n_groups_a, axis=2)

    ffn1 = (row(ffn1_norm), bf(ffn1_w_gate), bf(ffn1_w_up), bf(ffn1_w_down))
    ffn2 = (row(ffn2_norm), bf(ffn2_w_gate), bf(ffn2_w_up), bf(ffn2_w_down))
    mix_g, w_in_b = row(mix_norm), bf(w_in)
    mixer_params = (row(sgu_norm_g), row(sgu_norm_b), sgu_w_masked, sgu_bias, conv_w, bf(pool_w),
                    row(pool_scale), bf(w_branch_a), bf(w_branch_b), bf(w_branch_c), bf(w_out))
    ple_g, ple_wg, ple_wp = row(ple_norm), bf(ple_w_gate), bf(ple_w_proj)
    final_g = final_norm[None, :]

    tile = lambda a, tf: bf(a).reshape(a.shape[0], a.shape[1], a.shape[2] // tf, tf).transpose(0, 2, 1, 3)
    ffn2_tiled = (ffn2[0], tile(ffn2_w_gate, 512), tile(ffn2_w_up, 512), ffn2[3])
    ffn1_arms = [dict(row_chunk=256, unroll=1), dict(row_chunk=512, unroll=1),
                 dict(tm=512, row_chunk=256, unroll=2), dict(tf=256, row_chunk=256, unroll=2)]
    ffn2_arms = [dict(row_chunk=256, unroll=2), dict(row_chunk=256, unroll=2, tiled=True),
                 dict(row_chunk=256, unroll=4), dict(row_chunk=256, unroll=2)]
    inproj_arms = [dict(tn=512, row_chunk=256, unroll=1), dict(tn=1024, row_chunk=512, unroll=2),
                   dict(tn=2048, row_chunk=512, unroll=2),
                   dict(tn=1024, row_chunk=512, unroll=2, tanh_sigmoid=True)]
    mixer_arms = [dict(tm=256), dict(tm=128), dict(tm=256), dict(tm=256)]
    ple_arms = [dict(tm=256), dict(tm=512), dict(tm=128), dict(tm=256)]

    h = x.reshape(t, d)
    p2 = p.reshape(depth * t, p.shape[-1])
    for layer in range(depth):
        h = _ffn(h, *ffn1, layer, name=f"ffn1_L{layer}", **ffn1_arms[layer])
        z = _inproj(h, mix_g, w_in_b, layer, 2 * d_a, 3 * d_b + d_c, name=f"inproj_L{layer}",
                    **inproj_arms[layer])
        h = _mixer(z, h, seq, *mixer_params, layer, name=f"mixer_L{layer}", **mixer_arms[layer])
        ffn2_w = ffn2_tiled if ffn2_arms[layer].get("tiled") else ffn2
        h = _ffn(h, *ffn2_w, layer, name=f"ffn2_L{layer}", **ffn2_arms[layer])
        h = _ple(h, p2, ple_g, ple_wg, ple_wp, final_g, layer, layer == depth - 1, name=f"ple_L{layer}",
                 **ple_arms[layer])
    return h.reshape(bsz, seq, d)
```

```python
import functools

import jax
import jax.numpy as jnp
from jax import lax
from jax.experimental import pallas as pl
from jax.experimental.pallas import tpu as pltpu

EPS = 1e-6
CHUNK = 64
SGU_BLOCK = 128
CONV_W = 3
POOL_WINDOWS = (2, 4, 8, 16)
HALO = 16

V7X_VMEM_BYTES = 64 * 1024 * 1024
VMEM_LIMIT_BYTES = V7X_VMEM_BYTES - 6 * 1024 * 1024

F32 = jnp.float32
BF16 = jnp.bfloat16


def _pick(pref, n):
    t = min(pref, n)
    assert n % t == 0, (pref, n)
    return t


def _rmsnorm(x, g):
    return x * lax.rsqrt(jnp.mean(x * x, axis=-1, keepdims=True) + EPS) * g


def _gelu(x):
    return 0.5 * x * (1.0 + lax.erf(x * (2.0 ** -0.5)))


def _for_rows(total, chunk, body, unroll=1):
    n = total // chunk
    if n == unroll:
        for r in range(n):
            body(pl.ds(r * chunk, chunk))
        return

    def step(r, carry):
        body(pl.ds(pl.multiple_of(r * chunk, chunk), chunk))
        return carry

    lax.fori_loop(0, n, step, 0, unroll=unroll)


def _ffn_kernel(*refs, row_chunk, unroll, merged, tanh_silu, n_ride):
    n_w = 2 if merged else 3
    x_ref, g_ref = refs[:2]
    w_refs = refs[2:2 + n_w]
    ride_in = refs[2 + n_w:2 + n_w + n_ride]
    o_ref = refs[2 + n_w + n_ride]
    ride_out = refs[3 + n_w + n_ride:3 + n_w + 2 * n_ride]
    xn_ref = refs[-1]
    tm = x_ref.shape[0]
    tf = w_refs[-1].shape[0]

    for src, dst in zip(ride_in, ride_out):
        dst[...] = src[...].astype(BF16)

    @pl.when(pl.program_id(1) == 0)
    def _():
        def init(rows):
            x = x_ref[rows, :]
            xn_ref[rows, :] = _rmsnorm(x, g_ref[...]).astype(BF16)
            o_ref[rows, :] = x
        _for_rows(tm, min(row_chunk, 256), init)

    def step(rows):
        xn = xn_ref[rows, :]
        if merged:
            gu = jnp.dot(xn, w_refs[0][...], preferred_element_type=F32)
            gate, up = gu[:, :tf], gu[:, tf:]
        else:
            gate = jnp.dot(xn, w_refs[0][...], preferred_element_type=F32)
            up = jnp.dot(xn, w_refs[1][...], preferred_element_type=F32)
        if tanh_silu:
            half = 0.5 * gate
            act = (0.5 * (half + half * jnp.tanh(half)) * up).astype(BF16)
        else:
            act = (0.5 * jax.nn.silu(gate) * up).astype(BF16)
        o_ref[rows, :] += jnp.dot(act, w_refs[-1][...], preferred_element_type=F32)
    _for_rows(tm, row_chunk, step, unroll=unroll)


def _ffn(x, norm_g, weights, layer, ride=None, ride_layer=0, *, name, tm=1024, tf=512, row_chunk=256, unroll=1,
         merged=False, tanh_silu=False):
    t, d = x.shape
    f = weights[-1].shape[0]
    tm = _pick(tm, t)
    tf = _pick(tf, f)
    row_chunk = _pick(row_chunk, tm)
    n_i, n_f = t // tm, f // tf
    if merged:
        w_specs = [pl.BlockSpec((d, 2 * tf), lambda i, j: (0, j))]
    else:
        w_specs = [pl.BlockSpec((d, tf), lambda i, j: (0, j))] * 2
    w_specs.append(pl.BlockSpec((tf, d), lambda i, j: (j, 0)))
    ride_in_specs, ride_out_specs, ride_shapes = [], [], []
    if ride is not None:
        assert d % n_i == 0
        rg = d // n_i
        ride_in_specs = [pl.BlockSpec((None, rg, tf), lambda i, j: (ride_layer, i, j))] * 2 + [
            pl.BlockSpec((None, tf, rg), lambda i, j: (ride_layer, j, i))]
        ride_out_specs = [pl.BlockSpec((rg, tf), lambda i, j: (i, j))] * 2 + [
            pl.BlockSpec((tf, rg), lambda i, j: (j, i))]
        ride_shapes = [jax.ShapeDtypeStruct((d, f), BF16)] * 2 + [jax.ShapeDtypeStruct((f, d), BF16)]
    out = pl.pallas_call(
        functools.partial(_ffn_kernel, row_chunk=row_chunk, unroll=unroll, merged=merged, tanh_silu=tanh_silu,
                          n_ride=len(ride_shapes)),
        grid=(n_i, n_f),
        in_specs=[pl.BlockSpec((tm, d), lambda i, j: (i, 0)),
                  pl.BlockSpec((None, 1, d), lambda i, j: (layer, 0, 0))] + w_specs + ride_in_specs,
        out_specs=[pl.BlockSpec((tm, d), lambda i, j: (i, 0))] + ride_out_specs,
        out_shape=[jax.ShapeDtypeStruct((t, d), F32)] + ride_shapes,
        scratch_shapes=[pltpu.VMEM((tm, d), BF16)],
        compiler_params=pltpu.CompilerParams(
            dimension_semantics=("arbitrary", "arbitrary"), vmem_limit_bytes=VMEM_LIMIT_BYTES),
        name=name,
    )(x, norm_g, *weights, *(ride or ()))
    return out[0], tuple(out[1:])


def _inproj_kernel(h_ref, g_ref, w_ref, z_ref, xn_ref, *, row_chunk, gelu_tiles, linear_tiles, unroll,
                   tanh_sigmoid):
    tm = h_ref.shape[0]
    j = pl.program_id(1)

    @pl.when(j == 0)
    def _():
        def init(rows):
            xn_ref[rows, :] = _rmsnorm(h_ref[rows, :], g_ref[...]).astype(BF16)
        _for_rows(tm, min(row_chunk, 256), init)

    def project(act):
        def step(rows):
            z = jnp.dot(xn_ref[rows, :], w_ref[...], preferred_element_type=F32)
            z_ref[rows, :] = act(z).astype(z_ref.dtype)
        _for_rows(tm, row_chunk, step, unroll=unroll)

    @pl.when(j < gelu_tiles)
    def _():
        project(_gelu)

    @pl.when((j >= gelu_tiles) & (j < gelu_tiles + linear_tiles))
    def _():
        project(lambda z: z)

    @pl.when(j >= gelu_tiles + linear_tiles)
    def _():
        project((lambda z: 0.5 + 0.5 * jnp.tanh(0.5 * z)) if tanh_sigmoid else jax.nn.sigmoid)


def _inproj(h, norm_g, w_in, layer, n_gelu, n_linear, *, name, tm=1024, tn=1024, row_chunk=512, unroll=2,
            tanh_sigmoid=False):
    t, d = h.shape
    n = w_in.shape[-1]
    tm = _pick(tm, t)
    tn = _pick(tn, n)
    row_chunk = _pick(row_chunk, tm)
    assert n_gelu % tn == 0 and n_linear % tn == 0
    return pl.pallas_call(
        functools.partial(_inproj_kernel, row_chunk=row_chunk, gelu_tiles=n_gelu // tn,
                          linear_tiles=n_linear // tn, unroll=unroll, tanh_sigmoid=tanh_sigmoid),
        grid=(t // tm, n // tn),
        in_specs=[
            pl.BlockSpec((tm, d), lambda i, j: (i, 0)),
            pl.BlockSpec((None, 1, d), lambda i, j: (layer, 0, 0)),
            pl.BlockSpec((None, d, tn), lambda i, j: (layer, 0, j)),
        ],
        out_specs=pl.BlockSpec((tm, tn), lambda i, j: (i, j)),
        out_shape=jax.ShapeDtypeStruct((t, n), BF16),
        scratch_shapes=[pltpu.VMEM((tm, d), BF16)],
        compiler_params=pltpu.CompilerParams(
            dimension_semantics=("parallel", "arbitrary"), vmem_limit_bytes=VMEM_LIMIT_BYTES),
        name=name,
    )(h, norm_g, w_in)


def _mixer_kernel(z_ref, h_ref, lng_ref, lnb_ref, sguw_ref, sgub_ref, convw_ref, poolw_ref, pscale_ref,
                  wa_ref, wb_ref, wc_ref, wout_ref, o_ref, ybuf_ref, xbuf_ref, *, tiles_per_seq, d_a, d_b, d_c):
    tm, d = h_ref.shape
    n_groups_a = sguw_ref.shape[0]
    dg_a = d_a // n_groups_a
    dg_c = d_c // len(POOL_WINDOWS)
    o1 = 2 * d_a
    o2 = o1 + 3 * d_b
    o3 = o2 + d_c
    tile_in_seq = pl.program_id(0) % tiles_per_seq

    @pl.when(tile_in_seq == 0)
    def _():
        ybuf_ref[0:HALO, :] = jnp.zeros((HALO, d_b), F32)
        xbuf_ref[0:HALO, :] = jnp.zeros((HALO, d_c), F32)

    v = z_ref[:, d_a:o1].astype(F32)
    mu = jnp.mean(v, axis=-1, keepdims=True)
    vc = v - mu
    var = jnp.mean(vc * vc, axis=-1, keepdims=True)
    vn = (vc * lax.rsqrt(var + EPS) * lng_ref[...] + lnb_ref[...]).astype(BF16)
    blocks = []
    for blk in range(tm // SGU_BLOCK):
        rows = slice(blk * SGU_BLOCK, (blk + 1) * SGU_BLOCK)
        groups = [jnp.dot(sguw_ref[g], vn[rows, g * dg_a:(g + 1) * dg_a], preferred_element_type=F32)
                  for g in range(n_groups_a)]
        blocks.append(jnp.concatenate(groups, axis=1) + sgub_ref[...])
    vm = jnp.concatenate(blocks, axis=0) if len(blocks) > 1 else blocks[0]
    ya = (z_ref[:, 0:d_a].astype(F32) * vm).astype(BF16)

    ybuf_ref[HALO:HALO + tm, :] = (z_ref[:, o1 + d_b:o1 + 2 * d_b].astype(F32)
                                   * z_ref[:, o1 + 2 * d_b:o2].astype(F32))
    conv = convw_ref[CONV_W - 1:CONV_W, :] * ybuf_ref[HALO:HALO + tm, :]
    for k in range(CONV_W - 1):
        back = CONV_W - 1 - k
        conv += convw_ref[k:k + 1, :] * ybuf_ref[HALO - back:HALO - back + tm, :]
    yb = (z_ref[:, o1:o1 + d_b].astype(F32) * conv).astype(BF16)
    ybuf_ref[0:HALO, :] = ybuf_ref[tm:tm + HALO, :]

    xbuf_ref[HALO:HALO + tm, :] = z_ref[:, o2:o3].astype(F32)
    frames = (tile_in_seq * tm + 1 + lax.broadcasted_iota(jnp.int32, (tm, 1), 0)).astype(F32)
    yc_groups = []
    for g, w in enumerate(POOL_WINDOWS):
        cols = slice(g * dg_c, (g + 1) * dg_c)
        cur = xbuf_ref[HALO:HALO + tm, cols]
        total = cur
        for back in range(1, w):
            total += xbuf_ref[HALO - back:HALO - back + tm, cols]
        pooled = (total / jnp.minimum(frames, F32(w)) - cur).astype(BF16)
        yc_groups.append(jnp.dot(pooled, poolw_ref[g], preferred_element_type=F32))
    yc = (jnp.concatenate(yc_groups, axis=1) * pscale_ref[...]).astype(BF16)
    xbuf_ref[0:HALO, :] = xbuf_ref[tm:tm + HALO, :]

    m = z_ref[:, o3:o3 + d].astype(F32) * jnp.dot(ya, wa_ref[...], preferred_element_type=F32)
    m += z_ref[:, o3 + d:o3 + 2 * d].astype(F32) * jnp.dot(yb, wb_ref[...], preferred_element_type=F32)
    m += z_ref[:, o3 + 2 * d:o3 + 3 * d].astype(F32) * jnp.dot(yc, wc_ref[...], preferred_element_type=F32)
    o_ref[...] = h_ref[...] + jnp.dot(m.astype(BF16), wout_ref[...], preferred_element_type=F32)


def _resident(shape, layer):
    zeros = (0,) * len(shape)
    return pl.BlockSpec((None,) + tuple(shape), lambda i: (layer,) + zeros, pipeline_mode=pl.Buffered(1))


def _mixer(z, h, seq, ln_g, ln_b, sgu_w, sgu_bias, conv_w, pool_w, pool_scale, w_a, w_b, w_c, w_out, layer,
           *, name, tm=256):
    t, d = h.shape
    n_in = z.shape[-1]
    d_a, d_b, d_c = w_a.shape[1], w_b.shape[1], w_c.shape[1]
    tm = _pick(tm, seq)
    assert tm % SGU_BLOCK == 0 and tm >= HALO
    params = (ln_g, ln_b, sgu_w, sgu_bias, conv_w, pool_w, pool_scale, w_a, w_b, w_c, w_out)
    return pl.pallas_call(
        functools.partial(_mixer_kernel, tiles_per_seq=seq // tm, d_a=d_a, d_b=d_b, d_c=d_c),
        grid=(t // tm,),
        in_specs=[pl.BlockSpec((tm, n_in), lambda i: (i, 0)), pl.BlockSpec((tm, d), lambda i: (i, 0))]
                 + [_resident(a.shape[1:], layer) for a in params],
        out_specs=pl.BlockSpec((tm, d), lambda i: (i, 0)),
        out_shape=jax.ShapeDtypeStruct((t, d), F32),
        scratch_shapes=[pltpu.VMEM((HALO + tm, d_b), F32), pltpu.VMEM((HALO + tm, d_c), F32)],
        compiler_params=pltpu.CompilerParams(
            dimension_semantics=("arbitrary",), vmem_limit_bytes=VMEM_LIMIT_BYTES),
        name=name,
    )(z, h, *params)


def _ple_kernel(h_ref, p_ref, g_ref, wg_ref, wp_ref, fg_ref, o_ref, *, final_norm):
    h = h_ref[...]
    gate = jnp.dot(_rmsnorm(h, g_ref[...]).astype(BF16), wg_ref[...], preferred_element_type=F32)
    proj = jnp.dot(p_ref[...].astype(BF16), wp_ref[...], preferred_element_type=F32)
    out = h + jax.nn.sigmoid(gate) * proj
    if final_norm:
        out = _rmsnorm(out, fg_ref[...])
    o_ref[...] = out


def _ple(h, p, norm_g, w_gate, w_proj, final_g, layer, final_norm, *, name, tm=256):
    t, d = h.shape
    d_ple = p.shape[-1]
    tm = _pick(tm, t)
    tiles = t // tm
    return pl.pallas_call(
        functools.partial(_ple_kernel, final_norm=final_norm),
        grid=(tiles,),
        in_specs=[
            pl.BlockSpec((tm, d), lambda i: (i, 0)),
            pl.BlockSpec((tm, d_ple), lambda i: (layer * tiles + i, 0)),
            _resident((1, d), layer),
            _resident((d, d), layer),
            _resident((d_ple, d), layer),
            pl.BlockSpec((1, d), lambda i: (0, 0)),
        ],
        out_specs=pl.BlockSpec((tm, d), lambda i: (i, 0)),
        out_shape=jax.ShapeDtypeStruct((t, d), F32),
        compiler_params=pltpu.CompilerParams(
            dimension_semantics=("parallel",), vmem_limit_bytes=VMEM_LIMIT_BYTES),
        name=name,
    )(h, p, norm_g, w_gate, w_proj, final_g)


def kernel(x, p, ffn1_norm, ffn1_w_gate, ffn1_w_up, ffn1_w_down, mix_norm, w_in, sgu_norm_g, sgu_norm_b, sgu_w, sgu_b, conv_w, pool_w, pool_scale, w_branch_a, w_branch_b, w_branch_c, w_out, ffn2_norm, ffn2_w_gate, ffn2_w_up, ffn2_w_down, ple_norm, ple_w_gate, ple_w_proj, final_norm):
    bsz, seq, d = x.shape
    depth = p.shape[0]
    t = bsz * seq
    d_a = w_branch_a.shape[1]
    d_b = w_branch_b.shape[1]
    d_c = w_branch_c.shape[1]
    n_groups_a = sgu_w.shape[1]

    bf = lambda a: a.astype(BF16)
    row = lambda a: a[:, None, :]

    pos = jnp.arange(SGU_BLOCK)
    mask = (pos[None, :] // CHUNK) <= (pos[:, None] // CHUNK)
    sgu_w_masked = bf(jnp.where(mask[None, None], sgu_w, jnp.zeros_like(sgu_w)))
    sgu_bias = jnp.repeat(jnp.swapaxes(sgu_b, 1, 2), d_a // n_groups_a, axis=2)

    mix_g, w_in_b = row(mix_norm), bf(w_in)
    mixer_params = (row(sgu_norm_g), row(sgu_norm_b), sgu_w_masked, sgu_bias, conv_w, bf(pool_w),
                    row(pool_scale), bf(w_branch_a), bf(w_branch_b), bf(w_branch_c), bf(w_out))
    ple_g, ple_wg, ple_wp = row(ple_norm), bf(ple_w_gate), bf(ple_w_proj)
    final_g = final_norm[None, :]

    f_hidden = ffn1_w_gate.shape[-1]
    ffn1_f32 = (ffn1_w_gate, ffn1_w_up, ffn1_w_down)
    ffn2_f32 = (ffn2_w_gate, ffn2_w_up, ffn2_w_down)
    cast_layer = lambda ws, l: tuple(bf(w[l]) for w in ws)

    def merge(ws, tf=512):
        g, u, dn = ws
        gu = jnp.stack([g.reshape(d, f_hidden // tf, tf), u.reshape(d, f_hidden // tf, tf)], axis=2)
        return gu.reshape(d, 2 * f_hidden), dn

    ffn1_arms = [dict(row_chunk=512, unroll=1), dict(row_chunk=1024, unroll=1),
                 dict(row_chunk=512, unroll=2), dict(row_chunk=1024, unroll=1, merged=True)]
    ffn2_arms = [dict(row_chunk=512, unroll=1), dict(row_chunk=1024, unroll=1, tanh_silu=True),
                 dict(row_chunk=1024, unroll=1), dict(row_chunk=512, unroll=1)]
    ffn2_rides = [True, False, True, False]
    inproj_arms = [dict(tn=2048, row_chunk=512, unroll=2, tanh_sigmoid=True),
                   dict(tn=2048, row_chunk=1024, unroll=1, tanh_sigmoid=True),
                   dict(tn=2048, row_chunk=256, unroll=4, tanh_sigmoid=True),
                   dict(tn=1024, row_chunk=1024, unroll=1, tanh_sigmoid=True)]

    h = x.reshape(t, d)
    p2 = p.reshape(depth * t, p.shape[-1])
    ridden = None
    for layer in range(depth):
        w1 = ridden if ridden else cast_layer(ffn1_f32, layer)
        if ffn1_arms[layer].get("merged"):
            w1 = merge(w1)
        h, _ = _ffn(h, row(ffn1_norm), w1, layer, name=f"ffn1_L{layer}", **ffn1_arms[layer])
        z = _inproj(h, mix_g, w_in_b, layer, 2 * d_a, 3 * d_b + d_c, name=f"inproj_L{layer}",
                    **inproj_arms[layer])
        h = _mixer(z, h, seq, *mixer_params, layer, name=f"mixer_L{layer}", tm=256)
        ride = ffn1_f32 if (ffn2_rides[layer] and layer + 1 < depth) else None
        h, ridden = _ffn(h, row(ffn2_norm), cast_layer(ffn2_f32, layer), layer, ride, layer + 1,
                         name=f"ffn2_L{layer}", **ffn2_arms[layer])
        h = _ple(h, p2, ple_g, ple_wg, ple_wp, final_g, layer, layer == depth - 1, name=f"ple_L{layer}", tm=512)
    return h.reshape(bsz, seq, d)
```

```python
import functools

import jax
import jax.numpy as jnp
from jax import lax
from jax.experimental import pallas as pl
from jax.experimental.pallas import tpu as pltpu

EPS = 1e-6
CHUNK = 64
SGU_BLOCK = 128
CONV_W = 3
POOL_WINDOWS = (2, 4, 8, 16)
NORM_ROWS = 256
HALO = 16

V7X_VMEM_BYTES = 64 * 1024 * 1024
VMEM_LIMIT_BYTES = V7X_VMEM_BYTES - 6 * 1024 * 1024

F32 = jnp.float32
BF16 = jnp.bfloat16


def _pick(pref, n):
    t = min(pref, n)
    assert n % t == 0, (pref, n)
    return t


def _rmsnorm(x, g):
    return x * lax.rsqrt(jnp.mean(x * x, axis=-1, keepdims=True) + EPS) * g


def _gelu(x):
    return 0.5 * x * (1.0 + lax.erf(x * (2.0 ** -0.5)))


def _for_rows(total, chunk, body, unroll=1):
    n = total // chunk
    if n == unroll:
        for r in range(n):
            body(pl.ds(r * chunk, chunk))
        return

    def step(r, carry):
        body(pl.ds(pl.multiple_of(r * chunk, chunk), chunk))
        return carry

    lax.fori_loop(0, n, step, 0, unroll=unroll)


def _ffn_kernel(*refs, row_chunk, unroll, merged, tanh_silu, n_ride, fused_first):
    n_w = 2 if merged else 3
    x_ref, g_ref = refs[:2]
    w_refs = refs[2:2 + n_w]
    ride_in = refs[2 + n_w:2 + n_w + n_ride]
    o_ref = refs[2 + n_w + n_ride]
    ride_out = refs[3 + n_w + n_ride:3 + n_w + 2 * n_ride]
    xn_ref = refs[-1]
    tm = x_ref.shape[0]
    tf = w_refs[-1].shape[0]

    for src, dst in zip(ride_in, ride_out):
        dst[...] = src[...].astype(BF16)

    def swiglu_down(xn):
        if merged:
            gu = jnp.dot(xn, w_refs[0][...], preferred_element_type=F32)
            gate, up = gu[:, :tf], gu[:, tf:]
        else:
            gate = jnp.dot(xn, w_refs[0][...], preferred_element_type=F32)
            up = jnp.dot(xn, w_refs[1][...], preferred_element_type=F32)
        if tanh_silu:
            half = 0.5 * gate
            act = (0.5 * (half + half * jnp.tanh(half)) * up).astype(BF16)
        else:
            act = (0.5 * jax.nn.silu(gate) * up).astype(BF16)
        return jnp.dot(act, w_refs[-1][...], preferred_element_type=F32)

    def step(rows):
        o_ref[rows, :] += swiglu_down(xn_ref[rows, :])

    def first_step(rows):
        x = x_ref[rows, :]
        xn = _rmsnorm(x, g_ref[...]).astype(BF16)
        xn_ref[rows, :] = xn
        o_ref[rows, :] = x + swiglu_down(xn)

    if fused_first:
        first = pl.program_id(1) == 0

        @pl.when(first)
        def _():
            _for_rows(tm, NORM_ROWS, first_step, unroll=tm // NORM_ROWS)

        @pl.when(jnp.logical_not(first))
        def _():
            _for_rows(tm, row_chunk, step, unroll=unroll)
        return

    @pl.when(pl.program_id(1) == 0)
    def _():
        def init(rows):
            x = x_ref[rows, :]
            xn_ref[rows, :] = _rmsnorm(x, g_ref[...]).astype(BF16)
            o_ref[rows, :] = x
        _for_rows(tm, NORM_ROWS, init)

    _for_rows(tm, row_chunk, step, unroll=unroll)


def _ffn(x, norm_g, weights, layer, ride=None, ride_layer=0, *, name, tm=1024, tf=512, row_chunk=256, unroll=1,
         merged=False, tanh_silu=False, fused_first=False):
    t, d = x.shape
    f = weights[-1].shape[0]
    tm = _pick(tm, t)
    tf = _pick(tf, f)
    row_chunk = _pick(row_chunk, tm)
    n_i, n_f = t // tm, f // tf
    if merged:
        w_specs = [pl.BlockSpec((d, 2 * tf), lambda i, j: (0, j))]
    else:
        w_specs = [pl.BlockSpec((d, tf), lambda i, j: (0, j))] * 2
    w_specs.append(pl.BlockSpec((tf, d), lambda i, j: (j, 0)))
    ride_in_specs, ride_out_specs, ride_shapes = [], [], []
    if ride is not None:
        assert d % n_i == 0
        rg = d // n_i
        ride_in_specs = [pl.BlockSpec((None, rg, tf), lambda i, j: (ride_layer, i, j))] * 2 + [
            pl.BlockSpec((None, tf, rg), lambda i, j: (ride_layer, j, i))]
        ride_out_specs = [pl.BlockSpec((rg, tf), lambda i, j: (i, j))] * 2 + [
            pl.BlockSpec((tf, rg), lambda i, j: (j, i))]
        ride_shapes = [jax.ShapeDtypeStruct((d, f), BF16)] * 2 + [jax.ShapeDtypeStruct((f, d), BF16)]
    out = pl.pallas_call(
        functools.partial(_ffn_kernel, row_chunk=row_chunk, unroll=unroll, merged=merged, tanh_silu=tanh_silu,
                          n_ride=len(ride_shapes), fused_first=fused_first),
        grid=(n_i, n_f),
        in_specs=[pl.BlockSpec((tm, d), lambda i, j: (i, 0)),
                  pl.BlockSpec((None, 1, d), lambda i, j: (layer, 0, 0))] + w_specs + ride_in_specs,
        out_specs=[pl.BlockSpec((tm, d), lambda i, j: (i, 0))] + ride_out_specs,
        out_shape=[jax.ShapeDtypeStruct((t, d), F32)] + ride_shapes,
        scratch_shapes=[pltpu.VMEM((tm, d), BF16)],
        compiler_params=pltpu.CompilerParams(
            dimension_semantics=("arbitrary", "arbitrary"), vmem_limit_bytes=VMEM_LIMIT_BYTES),
        name=name,
    )(x, norm_g, *weights, *(ride or ()))
    return out[0], tuple(out[1:])


def _inproj_kernel(h_ref, g_ref, w_ref, z_ref, xn_ref, *, row_chunk, gelu_tiles, linear_tiles, unroll,
                   tanh_sigmoid):
    tm = h_ref.shape[0]
    j = pl.program_id(1)

    @pl.when(j == 0)
    def _():
        def init(rows):
            xn_ref[rows, :] = _rmsnorm(h_ref[rows, :], g_ref[...]).astype(BF16)
        _for_rows(tm, NORM_ROWS, init)

    def project(act):
        def step(rows):
            z = jnp.dot(xn_ref[rows, :], w_ref[...], preferred_element_type=F32)
            z_ref[rows, :] = act(z).astype(z_ref.dtype)
        _for_rows(tm, row_chunk, step, unroll=unroll)

    @pl.when(j < gelu_tiles)
    def _():
        project(_gelu)

    @pl.when((j >= gelu_tiles) & (j < gelu_tiles + linear_tiles))
    def _():
        project(lambda z: z)

    @pl.when(j >= gelu_tiles + linear_tiles)
    def _():
        project((lambda z: 0.5 + 0.5 * jnp.tanh(0.5 * z)) if tanh_sigmoid else jax.nn.sigmoid)


def _inproj(h, norm_g, w_in, layer, n_gelu, n_linear, *, name, tm=1024, tn=1024, row_chunk=512, unroll=2,
            tanh_sigmoid=False):
    t, d = h.shape
    n = w_in.shape[-1]
    tm = _pick(tm, t)
    tn = _pick(tn, n)
    row_chunk = _pick(row_chunk, tm)
    assert n_gelu % tn == 0 and n_linear % tn == 0
    return pl.pallas_call(
        functools.partial(_inproj_kernel, row_chunk=row_chunk, gelu_tiles=n_gelu // tn,
                          linear_tiles=n_linear // tn, unroll=unroll, tanh_sigmoid=tanh_sigmoid),
        grid=(t // tm, n // tn),
        in_specs=[
            pl.BlockSpec((tm, d), lambda i, j: (i, 0)),
            pl.BlockSpec((None, 1, d), lambda i, j: (layer, 0, 0)),
            pl.BlockSpec((None, d, tn), lambda i, j: (layer, 0, j)),
        ],
        out_specs=pl.BlockSpec((tm, tn), lambda i, j: (i, j)),
        out_shape=jax.ShapeDtypeStruct((t, n), BF16),
        scratch_shapes=[pltpu.VMEM((tm, d), BF16)],
        compiler_params=pltpu.CompilerParams(
            dimension_semantics=("parallel", "arbitrary"), vmem_limit_bytes=VMEM_LIMIT_BYTES),
        name=name,
    )(h, norm_g, w_in)


def _mix_prologue(z_ref, lng_ref, lnb_ref, convw_ref, ybuf_ref, xbuf_ref, tile_in_seq, d_a, d_b, d_c):
    tm = z_ref.shape[0]
    dg_c = d_c // len(POOL_WINDOWS)
    o1 = 2 * d_a
    o2 = o1 + 3 * d_b
    o3 = o2 + d_c

    v = z_ref[:, d_a:o1].astype(F32)
    mu = jnp.mean(v, axis=-1, keepdims=True)
    vc = v - mu
    var = jnp.mean(vc * vc, axis=-1, keepdims=True)
    vn = (vc * lax.rsqrt(var + EPS) * lng_ref[...] + lnb_ref[...]).astype(BF16)

    ybuf_ref[HALO:HALO + tm, :] = (z_ref[:, o1 + d_b:o1 + 2 * d_b].astype(F32)
                                   * z_ref[:, o1 + 2 * d_b:o2].astype(F32))
    conv = convw_ref[CONV_W - 1:CONV_W, :] * ybuf_ref[HALO:HALO + tm, :]
    for k in range(CONV_W - 1):
        back = CONV_W - 1 - k
        conv += convw_ref[k:k + 1, :] * ybuf_ref[HALO - back:HALO - back + tm, :]
    yb = (z_ref[:, o1:o1 + d_b].astype(F32) * conv).astype(BF16)
    ybuf_ref[0:HALO, :] = ybuf_ref[tm:tm + HALO, :]

    xbuf_ref[HALO:HALO + tm, :] = z_ref[:, o2:o3].astype(F32)
    frames = (tile_in_seq * tm + 1 + lax.broadcasted_iota(jnp.int32, (tm, 1), 0)).astype(F32)
    pooled = []
    for g, w in enumerate(POOL_WINDOWS):
        cols = slice(g * dg_c, (g + 1) * dg_c)
        cur = xbuf_ref[HALO:HALO + tm, cols]
        total = cur
        for back in range(1, w):
            total += xbuf_ref[HALO - back:HALO - back + tm, cols]
        pooled.append((total / jnp.minimum(frames, F32(w)) - cur).astype(BF16))
    xbuf_ref[0:HALO, :] = xbuf_ref[tm:tm + HALO, :]
    return vn, yb, jnp.concatenate(pooled, axis=1)


def _zero_halo_at_sequence_start(tile_in_seq, ybuf_ref, xbuf_ref):
    @pl.when(tile_in_seq == 0)
    def _():
        ybuf_ref[0:HALO, :] = jnp.zeros((HALO, ybuf_ref.shape[1]), F32)
        xbuf_ref[0:HALO, :] = jnp.zeros((HALO, xbuf_ref.shape[1]), F32)


def _mix_matmuls(u, vn, pooled, sguw_ref, sgub_ref, poolw_ref, pscale_ref):
    tm, d_a = vn.shape
    n_groups_a = sguw_ref.shape[0]
    dg_a = d_a // n_groups_a
    dg_c = pooled.shape[1] // len(POOL_WINDOWS)
    blocks = []
    for blk in range(tm // SGU_BLOCK):
        rows = slice(blk * SGU_BLOCK, (blk + 1) * SGU_BLOCK)
        groups = [jnp.dot(sguw_ref[g], vn[rows, g * dg_a:(g + 1) * dg_a], preferred_element_type=F32)
                  for g in range(n_groups_a)]
        blocks.append(jnp.concatenate(groups, axis=1) + sgub_ref[...])
    vm = jnp.concatenate(blocks, axis=0) if len(blocks) > 1 else blocks[0]
    ya = (u.astype(F32) * vm).astype(BF16)
    yc_groups = [jnp.dot(pooled[:, g * dg_c:(g + 1) * dg_c], poolw_ref[g], preferred_element_type=F32)
                 for g in range(len(POOL_WINDOWS))]
    yc = (jnp.concatenate(yc_groups, axis=1) * pscale_ref[...]).astype(BF16)
    return ya, yc


def _merge(gates, ya, yb, yc, h, wa_ref, wb_ref, wc_ref, wout_ref):
    m = gates[1].astype(F32) * jnp.dot(yb, wb_ref[...], preferred_element_type=F32)
    m += gates[0].astype(F32) * jnp.dot(ya, wa_ref[...], preferred_element_type=F32)
    m += gates[2].astype(F32) * jnp.dot(yc, wc_ref[...], preferred_element_type=F32)
    return h + jnp.dot(m.astype(BF16), wout_ref[...], preferred_element_type=F32)


def _mixer_kernel(z_ref, h_ref, lng_ref, lnb_ref, sguw_ref, sgub_ref, convw_ref, poolw_ref, pscale_ref,
                  wa_ref, wb_ref, wc_ref, wout_ref, o_ref, ybuf_ref, xbuf_ref, *, tiles_per_seq, d_a, d_b, d_c):
    d = h_ref.shape[1]
    o3 = 2 * d_a + 3 * d_b + d_c
    tile_in_seq = pl.program_id(0) % tiles_per_seq
    _zero_halo_at_sequence_start(tile_in_seq, ybuf_ref, xbuf_ref)
    vn, yb, pooled = _mix_prologue(z_ref, lng_ref, lnb_ref, convw_ref, ybuf_ref, xbuf_ref, tile_in_seq,
                                   d_a, d_b, d_c)
    ya, yc = _mix_matmuls(z_ref[:, 0:d_a], vn, pooled, sguw_ref, sgub_ref, poolw_ref, pscale_ref)
    gates = [z_ref[:, o3 + k * d:o3 + (k + 1) * d] for k in range(3)]
    o_ref[...] = _merge(gates, ya, yb, yc, h_ref[...], wa_ref, wb_ref, wc_ref, wout_ref)


def _mixer_pipe_kernel(zmix_ref, g0_ref, g1_ref, g2_ref, h_ref, lng_ref, lnb_ref, sguw_ref, sgub_ref, convw_ref,
                       poolw_ref, pscale_ref, wa_ref, wb_ref, wc_ref, wout_ref, o_ref, ybuf_ref, xbuf_ref,
                       stage_ref, *, tiles_per_seq, n_tiles, d_a, d_b, d_c):
    s = pl.program_id(0)
    c1, c2, c3 = d_a, 2 * d_a, 2 * d_a + d_b

    tile_in_seq = jnp.minimum(s, n_tiles - 1) % tiles_per_seq

    @pl.when(s == 0)
    def _():
        stage_ref[...] = jnp.zeros(stage_ref.shape, BF16)

    _zero_halo_at_sequence_start(tile_in_seq, ybuf_ref, xbuf_ref)
    ya, yc = _mix_matmuls(stage_ref[:, 0:c1], stage_ref[:, c1:c2], stage_ref[:, c3:], sguw_ref, sgub_ref,
                          poolw_ref, pscale_ref)
    o_ref[...] = _merge([g0_ref[...], g1_ref[...], g2_ref[...]], ya, stage_ref[:, c2:c3], yc, h_ref[...],
                        wa_ref, wb_ref, wc_ref, wout_ref)
    vn, yb, pooled = _mix_prologue(zmix_ref, lng_ref, lnb_ref, convw_ref, ybuf_ref, xbuf_ref, tile_in_seq,
                                   d_a, d_b, d_c)
    stage_ref[:, 0:c1] = zmix_ref[:, 0:d_a]
    stage_ref[:, c1:c2] = vn
    stage_ref[:, c2:c3] = yb
    stage_ref[:, c3:] = pooled


def _resident(shape, layer):
    zeros = (0,) * len(shape)
    return pl.BlockSpec((None,) + tuple(shape), lambda i: (layer,) + zeros, pipeline_mode=pl.Buffered(1))


def _mixer(z, h, seq, ln_g, ln_b, sgu_w, sgu_bias, conv_w, pool_w, pool_scale, w_a, w_b, w_c, w_out, layer,
           *, name, tm=256, pipelined=False):
    t, d = h.shape
    n_in = z.shape[-1]
    d_a, d_b, d_c = w_a.shape[1], w_b.shape[1], w_c.shape[1]
    n_mix = 2 * d_a + 3 * d_b + d_c
    tm = _pick(tm, seq)
    n_tiles = t // tm
    assert tm % SGU_BLOCK == 0 and tm >= HALO
    params = (ln_g, ln_b, sgu_w, sgu_bias, conv_w, pool_w, pool_scale, w_a, w_b, w_c, w_out)
    halo_scratch = [pltpu.VMEM((HALO + tm, d_b), F32), pltpu.VMEM((HALO + tm, d_c), F32)]
    common = dict(out_shape=jax.ShapeDtypeStruct((t, d), F32), name=name,
                  compiler_params=pltpu.CompilerParams(dimension_semantics=("arbitrary",),
                                                       vmem_limit_bytes=VMEM_LIMIT_BYTES))
    if not pipelined:
        return pl.pallas_call(
            functools.partial(_mixer_kernel, tiles_per_seq=seq // tm, d_a=d_a, d_b=d_b, d_c=d_c),
            grid=(n_tiles,),
            in_specs=[pl.BlockSpec((tm, n_in), lambda i: (i, 0)), pl.BlockSpec((tm, d), lambda i: (i, 0))]
                     + [_resident(a.shape[1:], layer) for a in params],
            out_specs=pl.BlockSpec((tm, d), lambda i: (i, 0)),
            scratch_shapes=halo_scratch, **common)(z, h, *params)
    assert n_mix % d == 0 and n_in == n_mix + 3 * d
    mix_tile = lambda s: (jnp.minimum(s, n_tiles - 1), 0)
    prev_tile = lambda s: jnp.maximum(s - 1, 0)
    gate_spec = lambda k: pl.BlockSpec((tm, d), lambda s: (prev_tile(s), n_mix // d + k))
    return pl.pallas_call(
        functools.partial(_mixer_pipe_kernel, tiles_per_seq=seq // tm, n_tiles=n_tiles, d_a=d_a, d_b=d_b, d_c=d_c),
        grid=(n_tiles + 1,),
        in_specs=[pl.BlockSpec((tm, n_mix), mix_tile), gate_spec(0), gate_spec(1), gate_spec(2),
                  pl.BlockSpec((tm, d), lambda s: (prev_tile(s), 0))]
                 + [_resident(a.shape[1:], layer) for a in params],
        out_specs=pl.BlockSpec((tm, d), lambda s: (prev_tile(s), 0)),
        scratch_shapes=halo_scratch + [pltpu.VMEM((tm, 2 * d_a + d_b + d_c), BF16)], **common)(z, z, z, z, h, *params)


def _ple_kernel(h_ref, p_ref, g_ref, wg_ref, wp_ref, fg_ref, o_ref, *, final_norm):
    h = h_ref[...]
    gate = jnp.dot(_rmsnorm(h, g_ref[...]).astype(BF16), wg_ref[...], preferred_element_type=F32)
    proj = jnp.dot(p_ref[...].astype(BF16), wp_ref[...], preferred_element_type=F32)
    out = h + jax.nn.sigmoid(gate) * proj
    if final_norm:
        out = _rmsnorm(out, fg_ref[...])
    o_ref[...] = out


def _ple(h, p, norm_g, w_gate, w_proj, final_g, layer, final_norm, *, name, tm=256):
    t, d = h.shape
    d_ple = p.shape[-1]
    tm = _pick(tm, t)
    tiles = t // tm
    return pl.pallas_call(
        functools.partial(_ple_kernel, final_norm=final_norm),
        grid=(tiles,),
        in_specs=[
            pl.BlockSpec((tm, d), lambda i: (i, 0)),
            pl.BlockSpec((tm, d_ple), lambda i: (layer * tiles + i, 0)),
            _resident((1, d), layer),
            _resident((d, d), layer),
            _resident((d_ple, d), layer),
            pl.BlockSpec((1, d), lambda i: (0, 0)),
        ],
        out_specs=pl.BlockSpec((tm, d), lambda i: (i, 0)),
        out_shape=jax.ShapeDtypeStruct((t, d), F32),
        compiler_params=pltpu.CompilerParams(
            dimension_semantics=("parallel",), vmem_limit_bytes=VMEM_LIMIT_BYTES),
        name=name,
    )(h, p, norm_g, w_gate, w_proj, final_g)


def kernel(x, p, ffn1_norm, ffn1_w_gate, ffn1_w_up, ffn1_w_down, mix_norm, w_in, sgu_norm_g, sgu_norm_b, sgu_w, sgu_b, conv_w, pool_w, pool_scale, w_branch_a, w_branch_b, w_branch_c, w_out, ffn2_norm, ffn2_w_gate, ffn2_w_up, ffn2_w_down, ple_norm, ple_w_gate, ple_w_proj, final_norm):
    bsz, seq, d = x.shape
    depth = p.shape[0]
    t = bsz * seq
    d_a = w_branch_a.shape[1]
    d_b = w_branch_b.shape[1]
    d_c = w_branch_c.shape[1]
    n_groups_a = sgu_w.shape[1]

    bf = lambda a: a.astype(BF16)
    row = lambda a: a[:, None, :]

    pos = jnp.arange(SGU_BLOCK)
    mask = (pos[None, :] // CHUNK) <= (pos[:, None] // CHUNK)
    sgu_w_masked = bf(jnp.where(mask[None, None], sgu_w, jnp.zeros_like(sgu_w)))
    sgu_bias = jnp.repeat(jnp.swapaxes(sgu_b, 1, 2), d_a // n_groups_a, axis=2)

    mix_g, w_in_b = row(mix_norm), bf(w_in)
    mixer_params = (row(sgu_norm_g), row(sgu_norm_b), sgu_w_masked, sgu_bias, conv_w, bf(pool_w),
                    row(pool_scale), bf(w_branch_a), bf(w_branch_b), bf(w_branch_c), bf(w_out))
    ple_g, ple_wg, ple_wp = row(ple_norm), bf(ple_w_gate), bf(ple_w_proj)
    final_g = final_norm[None, :]

    ffn1_f32 = (ffn1_w_gate, ffn1_w_up, ffn1_w_down)
    ffn2_f32 = (ffn2_w_gate, ffn2_w_up, ffn2_w_down)
    cast_layer = lambda ws, l: tuple(bf(w[l]) for w in ws)

    base = dict(row_chunk=1024, unroll=1, tanh_silu=True)
    ffn1_arms = [dict(base), dict(base), dict(base, fused_first=True), dict(base, fused_first=True)]
    ffn2_arms = [dict(base), dict(base), dict(base, fused_first=True), dict(base, fused_first=True)]
    inproj_arms = [dict(tn=2048, row_chunk=1024, unroll=1, tanh_sigmoid=True)] * 4
    mixer_arms = [dict(pipelined=False), dict(pipelined=True), dict(pipelined=False), dict(pipelined=False)]

    h = x.reshape(t, d)
    p2 = p.reshape(depth * t, p.shape[-1])
    w1 = cast_layer(ffn1_f32, 0)
    for layer in range(depth):
        h, w2 = _ffn(h, row(ffn1_norm), w1, layer, ffn2_f32, layer, name=f"ffn1_L{layer}", **ffn1_arms[layer])
        z = _inproj(h, mix_g, w_in_b, layer, 2 * d_a, 3 * d_b + d_c, name=f"inproj_L{layer}",
                    **inproj_arms[layer])
        h = _mixer(z, h, seq, *mixer_params, layer, name=f"mixer_L{layer}", tm=256, **mixer_arms[layer])
        ride = ffn1_f32 if layer + 1 < depth else None
        h, w1 = _ffn(h, row(ffn2_norm), w2, layer, ride, layer + 1, name=f"ffn2_L{layer}", **ffn2_arms[layer])
        h = _ple(h, p2, ple_g, ple_wg, ple_wp, final_g, layer, layer == depth - 1, name=f"ple_L{layer}", tm=512)
    return h.reshape(bsz, seq, d)
```

```python
import functools

import jax
import jax.numpy as jnp
from jax import lax
from jax.experimental import pallas as pl
from jax.experimental.pallas import tpu as pltpu

EPS = 1e-6
CHUNK = 64
SGU_BLOCK = 128
CONV_W = 3
POOL_WINDOWS = (2, 4, 8, 16)
NORM_ROWS = 256
HALO = 16

V7X_VMEM_BYTES = 64 * 1024 * 1024
VMEM_LIMIT_BYTES = V7X_VMEM_BYTES - 4 * 1024 * 1024

F32 = jnp.float32
BF16 = jnp.bfloat16


def _pick(pref, n):
    t = min(pref, n)
    assert n % t == 0, (pref, n)
    return t


def _rmsnorm(x, g):
    return x * lax.rsqrt(jnp.mean(x * x, axis=-1, keepdims=True) + EPS) * g


def _gelu(x):
    return 0.5 * x * (1.0 + lax.erf(x * (2.0 ** -0.5)))


def _for_rows(total, chunk, body, unroll=1):
    n = total // chunk
    if n == unroll:
        for r in range(n):
            body(pl.ds(r * chunk, chunk))
        return

    def step(r, carry):
        body(pl.ds(pl.multiple_of(r * chunk, chunk), chunk))
        return carry

    lax.fori_loop(0, n, step, 0, unroll=unroll)


def _ffn_kernel(*refs, row_chunk, unroll, merged, tanh_silu, n_ride, fused_first):
    n_w = 2 if merged else 3
    x_ref, g_ref = refs[:2]
    w_refs = refs[2:2 + n_w]
    ride_in = refs[2 + n_w:2 + n_w + n_ride]
    o_ref = refs[2 + n_w + n_ride]
    ride_out = refs[3 + n_w + n_ride:3 + n_w + 2 * n_ride]
    xn_ref = refs[-1]
    tm = x_ref.shape[0]
    tf = w_refs[-1].shape[0]

    for src, dst in zip(ride_in, ride_out):
        dst[...] = src[...].astype(BF16)

    def swiglu_down(xn):
        if merged:
            gu = jnp.dot(xn, w_refs[0][...], preferred_element_type=F32)
            gate, up = gu[:, :tf], gu[:, tf:]
        else:
            gate = jnp.dot(xn, w_refs[0][...], preferred_element_type=F32)
            up = jnp.dot(xn, w_refs[1][...], preferred_element_type=F32)
        if tanh_silu:
            half = 0.5 * gate
            act = (0.5 * (half + half * jnp.tanh(half)) * up).astype(BF16)
        else:
            act = (0.5 * jax.nn.silu(gate) * up).astype(BF16)
        return jnp.dot(act, w_refs[-1][...], preferred_element_type=F32)

    def step(rows):
        o_ref[rows, :] += swiglu_down(xn_ref[rows, :])

    def first_step(rows):
        x = x_ref[rows, :]
        xn = _rmsnorm(x, g_ref[...]).astype(BF16)
        xn_ref[rows, :] = xn
        o_ref[rows, :] = x + swiglu_down(xn)

    if fused_first:
        first = pl.program_id(1) == 0

        @pl.when(first)
        def _():
            _for_rows(tm, NORM_ROWS, first_step, unroll=tm // NORM_ROWS)

        @pl.when(jnp.logical_not(first))
        def _():
            _for_rows(tm, row_chunk, step, unroll=unroll)
        return

    @pl.when(pl.program_id(1) == 0)
    def _():
        def init(rows):
            x = x_ref[rows, :]
            xn_ref[rows, :] = _rmsnorm(x, g_ref[...]).astype(BF16)
            o_ref[rows, :] = x
        _for_rows(tm, NORM_ROWS, init)

    _for_rows(tm, row_chunk, step, unroll=unroll)


def _ride_specs(shape, layer, n_i, n_j):
    r, c = shape
    best = None
    for rows_on_i in (True, False):
        n_r, n_c = (n_i, n_j) if rows_on_i else (n_j, n_i)
        for p_r in (n_r, n_r - 1, 1):
            for p_c in (n_c, n_c - 1, 1):
                if p_r < 1 or p_c < 1 or r % p_r or c % p_c or (r // p_r) % 16 or (c // p_c) % 128:
                    continue
                if (p_r not in (n_r, 1) and n_r != n_j) or (p_c not in (n_c, 1) and n_c != n_j):
                    continue
                size = (r // p_r) * (c // p_c)
                if best is None or size < best[0]:
                    best = (size, rows_on_i, p_r, p_c)
    assert best is not None, shape
    _, rows_on_i, p_r, p_c = best

    def index(i, j):
        gr, gc = (i, j) if rows_on_i else (j, i)
        return jnp.minimum(gr, p_r - 1), jnp.minimum(gc, p_c - 1)

    block = (r // p_r, c // p_c)
    return (pl.BlockSpec((None,) + block, lambda i, j: (layer,) + index(i, j)), pl.BlockSpec(block, index))


def _ffn(x, norm_g, weights, layer, ride=(), *, name, tm=1024, tf=512, row_chunk=256, unroll=1,
         merged=False, tanh_silu=False, fused_first=False):
    t, d = x.shape
    f = weights[-1].shape[0]
    tm = _pick(tm, t)
    tf = _pick(tf, f)
    row_chunk = _pick(row_chunk, tm)
    n_i, n_f = t // tm, f // tf
    if merged:
        w_specs = [pl.BlockSpec((d, 2 * tf), lambda i, j: (0, j))]
    else:
        w_specs = [pl.BlockSpec((d, tf), lambda i, j: (0, j))] * 2
    w_specs.append(pl.BlockSpec((tf, d), lambda i, j: (j, 0)))
    ride_specs = [_ride_specs(a.shape[1:], l, n_i, n_f) for a, l in ride]
    out = pl.pallas_call(
        functools.partial(_ffn_kernel, row_chunk=row_chunk, unroll=unroll, merged=merged, tanh_silu=tanh_silu,
                          n_ride=len(ride), fused_first=fused_first),
        grid=(n_i, n_f),
        in_specs=[pl.BlockSpec((tm, d), lambda i, j: (i, 0)),
                  pl.BlockSpec((None, 1, d), lambda i, j: (layer, 0, 0))] + w_specs + [s[0] for s in ride_specs],
        out_specs=[pl.BlockSpec((tm, d), lambda i, j: (i, 0))] + [s[1] for s in ride_specs],
        out_shape=[jax.ShapeDtypeStruct((t, d), F32)] + [jax.ShapeDtypeStruct(a.shape[1:], BF16) for a, _ in ride],
        scratch_shapes=[pltpu.VMEM((tm, d), BF16)],
        compiler_params=pltpu.CompilerParams(
            dimension_semantics=("arbitrary", "arbitrary"), vmem_limit_bytes=VMEM_LIMIT_BYTES),
        name=name,
    )(x, norm_g, *weights, *(a for a, _ in ride))
    return out[0], tuple(out[1:])


def _ffn_stream_kernel(x_ref, g_ref, wg_hbm, wu_hbm, wd_hbm, o_ref, xn_ref, wg_buf, wu_buf, wd_buf, sem, *,
                       n_f, tf, tanh_silu):
    i = pl.program_id(0)
    tm = x_ref.shape[0]
    slot_of = lambda f: 2 if f == 0 else f % 2

    def copies(f):
        s = slot_of(f)
        return (pltpu.make_async_copy(wg_hbm.at[:, pl.ds(f * tf, tf)], wg_buf.at[s], sem.at[0, s]),
                pltpu.make_async_copy(wu_hbm.at[:, pl.ds(f * tf, tf)], wu_buf.at[s], sem.at[1, s]),
                pltpu.make_async_copy(wd_hbm.at[pl.ds(f * tf, tf), :], wd_buf.at[s], sem.at[2, s]))

    @pl.when(i == 0)
    def _():
        for c in copies(0):
            c.start()

    def init(rows):
        x = x_ref[rows, :]
        xn_ref[rows, :] = _rmsnorm(x, g_ref[...]).astype(BF16)
        o_ref[rows, :] = x
    _for_rows(tm, NORM_ROWS, init, unroll=tm // NORM_ROWS)

    for f in range(n_f):
        for c in copies((f + 1) % n_f):
            c.start()
        for c in copies(f):
            c.wait()
        s = slot_of(f)
        xn = xn_ref[...]
        gate = jnp.dot(xn, wg_buf[s], preferred_element_type=F32)
        up = jnp.dot(xn, wu_buf[s], preferred_element_type=F32)
        if tanh_silu:
            half = 0.5 * gate
            act = (0.5 * (half + half * jnp.tanh(half)) * up).astype(BF16)
        else:
            act = (0.5 * jax.nn.silu(gate) * up).astype(BF16)
        o_ref[...] += jnp.dot(act, wd_buf[s], preferred_element_type=F32)

    @pl.when(i == pl.num_programs(0) - 1)
    def _():
        for c in copies(0):
            c.wait()


def _ffn_stream(x, norm_g, weights, layer, *, name, tm=1024, tf=512, tanh_silu=True):
    t, d = x.shape
    f = weights[-1].shape[0]
    tm = _pick(tm, t)
    tf = _pick(tf, f)
    n_f = f // tf
    any_spec = pl.BlockSpec(memory_space=pl.ANY)
    return pl.pallas_call(
        functools.partial(_ffn_stream_kernel, n_f=n_f, tf=tf, tanh_silu=tanh_silu),
        grid=(t // tm,),
        in_specs=[pl.BlockSpec((tm, d), lambda i: (i, 0)),
                  pl.BlockSpec((None, 1, d), lambda i: (layer, 0, 0)), any_spec, any_spec, any_spec],
        out_specs=pl.BlockSpec((tm, d), lambda i: (i, 0)),
        out_shape=jax.ShapeDtypeStruct((t, d), F32),
        scratch_shapes=[pltpu.VMEM((tm, d), BF16), pltpu.VMEM((3, d, tf), BF16), pltpu.VMEM((3, d, tf), BF16),
                        pltpu.VMEM((3, tf, d), BF16), pltpu.SemaphoreType.DMA((3, 3))],
        compiler_params=pltpu.CompilerParams(
            dimension_semantics=("arbitrary",), vmem_limit_bytes=VMEM_LIMIT_BYTES),
        name=name,
    )(x, norm_g, *weights)


def _inproj_kernel(h_ref, g_ref, w_ref, z_ref, xn_ref, *, row_chunk, gelu_tiles, linear_tiles, unroll,
                   tanh_sigmoid):
    tm = h_ref.shape[0]
    j = pl.program_id(1)

    @pl.when(j == 0)
    def _():
        def init(rows):
            xn_ref[rows, :] = _rmsnorm(h_ref[rows, :], g_ref[...]).astype(BF16)
        _for_rows(tm, NORM_ROWS, init)

    def project(act):
        def step(rows):
            z = jnp.dot(xn_ref[rows, :], w_ref[...], preferred_element_type=F32)
            z_ref[rows, :] = act(z).astype(z_ref.dtype)
        _for_rows(tm, row_chunk, step, unroll=unroll)

    @pl.when(j < gelu_tiles)
    def _():
        project(_gelu)

    @pl.when((j >= gelu_tiles) & (j < gelu_tiles + linear_tiles))
    def _():
        project(lambda z: z)

    @pl.when(j >= gelu_tiles + linear_tiles)
    def _():
        project((lambda z: 0.5 + 0.5 * jnp.tanh(0.5 * z)) if tanh_sigmoid else jax.nn.sigmoid)


def _inproj(h, norm_g, w_in, layer, n_gelu, n_linear, *, name, tm=1024, tn=1024, row_chunk=512, unroll=2,
            tanh_sigmoid=False):
    t, d = h.shape
    n = w_in.shape[-1]
    tm = _pick(tm, t)
    tn = _pick(tn, n)
    row_chunk = _pick(row_chunk, tm)
    assert n_gelu % tn == 0 and n_linear % tn == 0
    return pl.pallas_call(
        functools.partial(_inproj_kernel, row_chunk=row_chunk, gelu_tiles=n_gelu // tn,
                          linear_tiles=n_linear // tn, unroll=unroll, tanh_sigmoid=tanh_sigmoid),
        grid=(t // tm, n // tn),
        in_specs=[
            pl.BlockSpec((tm, d), lambda i, j: (i, 0)),
            pl.BlockSpec((None, 1, d), lambda i, j: (layer, 0, 0)),
            pl.BlockSpec((d, tn), lambda i, j: (0, j)),
        ],
        out_specs=pl.BlockSpec((tm, tn), lambda i, j: (i, j)),
        out_shape=jax.ShapeDtypeStruct((t, n), BF16),
        scratch_shapes=[pltpu.VMEM((tm, d), BF16)],
        compiler_params=pltpu.CompilerParams(
            dimension_semantics=("parallel", "arbitrary"), vmem_limit_bytes=VMEM_LIMIT_BYTES),
        name=name,
    )(h, norm_g, w_in)


def _mix_prologue(z_ref, lng_ref, lnb_ref, convw_ref, ybuf_ref, xbuf_ref, tile_in_seq, d_a, d_b, d_c):
    tm = z_ref.shape[0]
    dg_c = d_c // len(POOL_WINDOWS)
    o1 = 2 * d_a
    o2 = o1 + 3 * d_b
    o3 = o2 + d_c

    v = z_ref[:, d_a:o1].astype(F32)
    mu = jnp.mean(v, axis=-1, keepdims=True)
    vc = v - mu
    var = jnp.mean(vc * vc, axis=-1, keepdims=True)
    vn = (vc * lax.rsqrt(var + EPS) * lng_ref[...] + lnb_ref[...]).astype(BF16)

    ybuf_ref[HALO:HALO + tm, :] = (z_ref[:, o1 + d_b:o1 + 2 * d_b].astype(F32)
                                   * z_ref[:, o1 + 2 * d_b:o2].astype(F32))
    conv = convw_ref[CONV_W - 1:CONV_W, :] * ybuf_ref[HALO:HALO + tm, :]
    for k in range(CONV_W - 1):
        back = CONV_W - 1 - k
        conv += convw_ref[k:k + 1, :] * ybuf_ref[HALO - back:HALO - back + tm, :]
    yb = (z_ref[:, o1:o1 + d_b].astype(F32) * conv).astype(BF16)
    ybuf_ref[0:HALO, :] = ybuf_ref[tm:tm + HALO, :]

    xbuf_ref[HALO:HALO + tm, :] = z_ref[:, o2:o3].astype(F32)
    frames = (tile_in_seq * tm + 1 + lax.broadcasted_iota(jnp.int32, (tm, 1), 0)).astype(F32)
    pooled = []
    for g, w in enumerate(POOL_WINDOWS):
        cols = slice(g * dg_c, (g + 1) * dg_c)
        cur = xbuf_ref[HALO:HALO + tm, cols]
        total = cur
        for back in range(1, w):
            total += xbuf_ref[HALO - back:HALO - back + tm, cols]
        pooled.append((total / jnp.minimum(frames, F32(w)) - cur).astype(BF16))
    xbuf_ref[0:HALO, :] = xbuf_ref[tm:tm + HALO, :]
    return vn, yb, jnp.concatenate(pooled, axis=1)


def _zero_halo_at_sequence_start(tile_in_seq, ybuf_ref, xbuf_ref):
    @pl.when(tile_in_seq == 0)
    def _():
        ybuf_ref[0:HALO, :] = jnp.zeros((HALO, ybuf_ref.shape[1]), F32)
        xbuf_ref[0:HALO, :] = jnp.zeros((HALO, xbuf_ref.shape[1]), F32)


def _mix_matmuls(u, vn, pooled, sguw_ref, sgub_ref, poolw_ref, pscale_ref):
    tm, d_a = vn.shape
    n_groups_a = sguw_ref.shape[0]
    dg_a = d_a // n_groups_a
    dg_c = pooled.shape[1] // len(POOL_WINDOWS)
    blocks = []
    for blk in range(tm // SGU_BLOCK):
        rows = slice(blk * SGU_BLOCK, (blk + 1) * SGU_BLOCK)
        groups = [jnp.dot(sguw_ref[g], vn[rows, g * dg_a:(g + 1) * dg_a], preferred_element_type=F32)
                  for g in range(n_groups_a)]
        blocks.append(jnp.concatenate(groups, axis=1) + sgub_ref[...])
    vm = jnp.concatenate(blocks, axis=0) if len(blocks) > 1 else blocks[0]
    ya = (u.astype(F32) * vm).astype(BF16)
    yc_groups = [jnp.dot(pooled[:, g * dg_c:(g + 1) * dg_c], poolw_ref[g], preferred_element_type=F32)
                 for g in range(len(POOL_WINDOWS))]
    yc = (jnp.concatenate(yc_groups, axis=1) * pscale_ref[...]).astype(BF16)
    return ya, yc


def _merge(gates, ya, yb, yc, h, wa_ref, wb_ref, wc_ref, wout_ref):
    m = gates[1].astype(F32) * jnp.dot(yb, wb_ref[...], preferred_element_type=F32)
    m += gates[0].astype(F32) * jnp.dot(ya, wa_ref[...], preferred_element_type=F32)
    m += gates[2].astype(F32) * jnp.dot(yc, wc_ref[...], preferred_element_type=F32)
    return h + jnp.dot(m.astype(BF16), wout_ref[...], preferred_element_type=F32)


def _mixer_kernel(z_ref, h_ref, lng_ref, lnb_ref, sguw_ref, sgub_ref, convw_ref, poolw_ref, pscale_ref,
                  wa_ref, wb_ref, wc_ref, wout_ref, o_ref, ybuf_ref, xbuf_ref, *, tiles_per_seq, d_a, d_b, d_c):
    d = h_ref.shape[1]
    o3 = 2 * d_a + 3 * d_b + d_c
    tile_in_seq = pl.program_id(0) % tiles_per_seq
    _zero_halo_at_sequence_start(tile_in_seq, ybuf_ref, xbuf_ref)
    vn, yb, pooled = _mix_prologue(z_ref, lng_ref, lnb_ref, convw_ref, ybuf_ref, xbuf_ref, tile_in_seq,
                                   d_a, d_b, d_c)
    ya, yc = _mix_matmuls(z_ref[:, 0:d_a], vn, pooled, sguw_ref, sgub_ref, poolw_ref, pscale_ref)
    gates = [z_ref[:, o3 + k * d:o3 + (k + 1) * d] for k in range(3)]
    o_ref[...] = _merge(gates, ya, yb, yc, h_ref[...], wa_ref, wb_ref, wc_ref, wout_ref)


def _mixer_pipe_kernel(zmix_ref, g0_ref, g1_ref, g2_ref, h_ref, lng_ref, lnb_ref, sguw_ref, sgub_ref, convw_ref,
                       poolw_ref, pscale_ref, wa_ref, wb_ref, wc_ref, wout_ref, o_ref, ybuf_ref, xbuf_ref,
                       stage_ref, *, tiles_per_seq, n_tiles, d_a, d_b, d_c):
    s = pl.program_id(0)
    c1, c2, c3 = d_a, 2 * d_a, 2 * d_a + d_b

    tile_in_seq = jnp.minimum(s, n_tiles - 1) % tiles_per_seq

    @pl.when(s == 0)
    def _():
        stage_ref[...] = jnp.zeros(stage_ref.shape, BF16)

    _zero_halo_at_sequence_start(tile_in_seq, ybuf_ref, xbuf_ref)
    ya, yc = _mix_matmuls(stage_ref[:, 0:c1], stage_ref[:, c1:c2], stage_ref[:, c3:], sguw_ref, sgub_ref,
                          poolw_ref, pscale_ref)
    o_ref[...] = _merge([g0_ref[...], g1_ref[...], g2_ref[...]], ya, stage_ref[:, c2:c3], yc, h_ref[...],
                        wa_ref, wb_ref, wc_ref, wout_ref)
    vn, yb, pooled = _mix_prologue(zmix_ref, lng_ref, lnb_ref, convw_ref, ybuf_ref, xbuf_ref, tile_in_seq,
                                   d_a, d_b, d_c)
    stage_ref[:, 0:c1] = zmix_ref[:, 0:d_a]
    stage_ref[:, c1:c2] = vn
    stage_ref[:, c2:c3] = yb
    stage_ref[:, c3:] = pooled


def _resident(shape, layer):
    zeros = (0,) * len(shape)
    return pl.BlockSpec((None,) + tuple(shape), lambda i: (layer,) + zeros, pipeline_mode=pl.Buffered(1))


def _mixer(z, h, seq, ln_g, ln_b, sgu_w, sgu_bias, conv_w, pool_w, pool_scale, w_a, w_b, w_c, w_out, layer,
           *, name, tm=256, pipelined=False):
    t, d = h.shape
    n_in = z.shape[-1]
    d_a, d_b, d_c = w_a.shape[1], w_b.shape[1], w_c.shape[1]
    n_mix = 2 * d_a + 3 * d_b + d_c
    tm = _pick(tm, seq)
    n_tiles = t // tm
    assert tm % SGU_BLOCK == 0 and tm >= HALO
    params = (ln_g, ln_b, sgu_w, sgu_bias, conv_w, pool_w, pool_scale, w_a, w_b, w_c, w_out)
    halo_scratch = [pltpu.VMEM((HALO + tm, d_b), F32), pltpu.VMEM((HALO + tm, d_c), F32)]
    common = dict(out_shape=jax.ShapeDtypeStruct((t, d), F32), name=name,
                  compiler_params=pltpu.CompilerParams(dimension_semantics=("arbitrary",),
                                                       vmem_limit_bytes=VMEM_LIMIT_BYTES))
    if not pipelined:
        return pl.pallas_call(
            functools.partial(_mixer_kernel, tiles_per_seq=seq // tm, d_a=d_a, d_b=d_b, d_c=d_c),
            grid=(n_tiles,),
            in_specs=[pl.BlockSpec((tm, n_in), lambda i: (i, 0)), pl.BlockSpec((tm, d), lambda i: (i, 0))]
                     + [_resident(a.shape[1:], layer) for a in params],
            out_specs=pl.BlockSpec((tm, d), lambda i: (i, 0)),
            scratch_shapes=halo_scratch, **common)(z, h, *params)
    assert n_mix % d == 0 and n_in == n_mix + 3 * d
    mix_tile = lambda s: (jnp.minimum(s, n_tiles - 1), 0)
    prev_tile = lambda s: jnp.maximum(s - 1, 0)
    gate_spec = lambda k: pl.BlockSpec((tm, d), lambda s: (prev_tile(s), n_mix // d + k))
    return pl.pallas_call(
        functools.partial(_mixer_pipe_kernel, tiles_per_seq=seq // tm, n_tiles=n_tiles, d_a=d_a, d_b=d_b, d_c=d_c),
        grid=(n_tiles + 1,),
        in_specs=[pl.BlockSpec((tm, n_mix), mix_tile), gate_spec(0), gate_spec(1), gate_spec(2),
                  pl.BlockSpec((tm, d), lambda s: (prev_tile(s), 0))]
                 + [_resident(a.shape[1:], layer) for a in params],
        out_specs=pl.BlockSpec((tm, d), lambda s: (prev_tile(s), 0)),
        scratch_shapes=halo_scratch + [pltpu.VMEM((tm, 2 * d_a + d_b + d_c), BF16)], **common)(z, z, z, z, h, *params)


def _ple_kernel(h_ref, p_ref, g_ref, wg_ref, wp_ref, fg_ref, o_ref, *, final_norm):
    h = h_ref[...]
    gate = jnp.dot(_rmsnorm(h, g_ref[...]).astype(BF16), wg_ref[...], preferred_element_type=F32)
    proj = jnp.dot(p_ref[...].astype(BF16), wp_ref[...], preferred_element_type=F32)
    out = h + jax.nn.sigmoid(gate) * proj
    if final_norm:
        out = _rmsnorm(out, fg_ref[...])
    o_ref[...] = out


def _ple(h, p, norm_g, w_gate, w_proj, final_g, layer, final_norm, *, name, tm=256):
    t, d = h.shape
    d_ple = p.shape[-1]
    tm = _pick(tm, t)
    tiles = t // tm
    return pl.pallas_call(
        functools.partial(_ple_kernel, final_norm=final_norm),
        grid=(tiles,),
        in_specs=[
            pl.BlockSpec((tm, d), lambda i: (i, 0)),
            pl.BlockSpec((tm, d_ple), lambda i: (layer * tiles + i, 0)),
            _resident((1, d), layer),
            _resident((d, d), layer),
            _resident((d_ple, d), layer),
            pl.BlockSpec((1, d), lambda i: (0, 0)),
        ],
        out_specs=pl.BlockSpec((tm, d), lambda i: (i, 0)),
        out_shape=jax.ShapeDtypeStruct((t, d), F32),
        compiler_params=pltpu.CompilerParams(
            dimension_semantics=("parallel",), vmem_limit_bytes=VMEM_LIMIT_BYTES),
        name=name,
    )(h, p, norm_g, w_gate, w_proj, final_g)


def kernel(x, p, ffn1_norm, ffn1_w_gate, ffn1_w_up, ffn1_w_down, mix_norm, w_in, sgu_norm_g, sgu_norm_b, sgu_w, sgu_b, conv_w, pool_w, pool_scale, w_branch_a, w_branch_b, w_branch_c, w_out, ffn2_norm, ffn2_w_gate, ffn2_w_up, ffn2_w_down, ple_norm, ple_w_gate, ple_w_proj, final_norm):
    bsz, seq, d = x.shape
    depth = p.shape[0]
    t = bsz * seq
    d_a = w_branch_a.shape[1]
    d_b = w_branch_b.shape[1]
    d_c = w_branch_c.shape[1]
    n_groups_a = sgu_w.shape[1]

    bf = lambda a: a.astype(BF16)
    row = lambda a: a[:, None, :]

    pos = jnp.arange(SGU_BLOCK)
    mask = (pos[None, :] // CHUNK) <= (pos[:, None] // CHUNK)
    sgu_w_masked = bf(jnp.where(mask[None, None], sgu_w, jnp.zeros_like(sgu_w)))
    sgu_bias = jnp.repeat(jnp.swapaxes(sgu_b, 1, 2), d_a // n_groups_a, axis=2)

    mix_g = row(mix_norm)
    mixer_params = (row(sgu_norm_g), row(sgu_norm_b), sgu_w_masked, sgu_bias, conv_w, bf(pool_w),
                    row(pool_scale), bf(w_branch_a), bf(w_branch_b), bf(w_branch_c), bf(w_out))
    ple_g, ple_wg, ple_wp = row(ple_norm), bf(ple_w_gate), bf(ple_w_proj)
    final_g = final_norm[None, :]

    ffn1_f32 = (ffn1_w_gate, ffn1_w_up, ffn1_w_down)
    ffn2_f32 = (ffn2_w_gate, ffn2_w_up, ffn2_w_down)
    cast_layer = lambda ws, l: tuple(bf(w[l]) for w in ws)

    base = dict(row_chunk=1024, unroll=1, tanh_silu=True)
    ffn1_arms = [dict(base)] * 4
    ffn2_arms = [dict(base)] * 4
    inproj_arms = [dict(tn=2048, row_chunk=1024, unroll=1, tanh_sigmoid=True)] * 4
    mixer_arms = [dict(pipelined=False)] * 4

    h = x.reshape(t, d)
    p2 = p.reshape(depth * t, p.shape[-1])
    w1 = cast_layer(ffn1_f32, 0)
    w_in_l = bf(w_in[0])
    for layer in range(depth):
        ride = [(w, layer) for w in ffn2_f32]
        if layer + 1 < depth:
            ride.append((w_in, layer + 1))
        h, cast = _ffn(h, row(ffn1_norm), w1, layer, ride, name=f"ffn1_L{layer}", **ffn1_arms[layer])
        w2, w_in_next = cast[:3], cast[3:]
        z = _inproj(h, mix_g, w_in_l, layer, 2 * d_a, 3 * d_b + d_c, name=f"inproj_L{layer}",
                    **inproj_arms[layer])
        h = _mixer(z, h, seq, *mixer_params, layer, name=f"mixer_L{layer}", tm=256, **mixer_arms[layer])
        if layer + 1 < depth:
            h, w1 = _ffn(h, row(ffn2_norm), w2, layer, [(w, layer + 1) for w in ffn1_f32],
                         name=f"ffn2_L{layer}", **ffn2_arms[layer])
            w_in_l = w_in_next[0]
        else:
            h = _ffn_stream(h, row(ffn2_norm), w2, layer, name=f"ffn2_L{layer}")
        h = _ple(h, p2, ple_g, ple_wg, ple_wp, final_g, layer, layer == depth - 1, name=f"ple_L{layer}", tm=512)
    return h.reshape(bsz, seq, d)
```

```python
import functools

import jax
import jax.numpy as jnp
from jax import lax
from jax.experimental import pallas as pl
from jax.experimental.pallas import tpu as pltpu

EPS = 1e-6
CHUNK = 64
SGU_BLOCK = 128
CONV_W = 3
POOL_WINDOWS = (2, 4, 8, 16)
HALO = 16

V7X_VMEM_BYTES = 64 * 1024 * 1024
VMEM_LIMIT_BYTES = V7X_VMEM_BYTES - 4 * 1024 * 1024
BF16_SUBLANES, LANES = 16, 128

FFN_ROWS, FFN_COLS = 1024, 512
INPROJ_ROWS, INPROJ_COLS = 1024, 2048
MIXER_ROWS = 256
PLE_ROWS = 512
NORM_ROWS = 256

F32 = jnp.float32
BF16 = jnp.bfloat16


def _pick(pref, n):
    t = min(pref, n)
    assert n % t == 0, (pref, n)
    return t


def _rmsnorm(x, g):
    return x * lax.rsqrt(jnp.mean(x * x, axis=-1, keepdims=True) + EPS) * g


def _gelu(x):
    return 0.5 * x * (1.0 + lax.erf(x * (2.0 ** -0.5)))


def _sigmoid(x):
    return 0.5 + 0.5 * jnp.tanh(0.5 * x)


def _for_rows(total, chunk, body):
    def step(r, carry):
        body(pl.ds(pl.multiple_of(r * chunk, chunk), chunk))
        return carry

    lax.fori_loop(0, total // chunk, step, 0)


def _compiler_params(*semantics):
    return pltpu.CompilerParams(dimension_semantics=semantics, vmem_limit_bytes=VMEM_LIMIT_BYTES)


def _ffn_kernel(*refs, n_ride):
    x_ref, g_ref, wg_ref, wu_ref, wd_ref = refs[:5]
    ride_in = refs[5:5 + n_ride]
    o_ref = refs[5 + n_ride]
    ride_out = refs[6 + n_ride:6 + 2 * n_ride]
    xn_ref = refs[-1]

    for src, dst in zip(ride_in, ride_out):
        dst[...] = src[...].astype(BF16)

    @pl.when(pl.program_id(1) == 0)
    def _():
        def init(rows):
            x = x_ref[rows, :]
            xn_ref[rows, :] = _rmsnorm(x, g_ref[...]).astype(BF16)
            o_ref[rows, :] = x
        _for_rows(x_ref.shape[0], NORM_ROWS, init)

    xn = xn_ref[...]
    gate = jnp.dot(xn, wg_ref[...], preferred_element_type=F32)
    up = jnp.dot(xn, wu_ref[...], preferred_element_type=F32)
    half = 0.5 * gate
    act = (0.5 * (half + half * jnp.tanh(half)) * up).astype(BF16)
    o_ref[...] += jnp.dot(act, wd_ref[...], preferred_element_type=F32)


def _ride_specs(shape, layer, n_i, n_j):
    r, c = shape
    best = None
    for rows_on_i in (True, False):
        n_r, n_c = (n_i, n_j) if rows_on_i else (n_j, n_i)
        for p_r in (n_r, n_r - 1, 1):
            for p_c in (n_c, n_c - 1, 1):
                if p_r < 1 or p_c < 1 or r % p_r or c % p_c or (r // p_r) % BF16_SUBLANES or (c // p_c) % LANES:
                    continue
                if (p_r not in (n_r, 1) and n_r != n_j) or (p_c not in (n_c, 1) and n_c != n_j):
                    continue
                size = (r // p_r) * (c // p_c)
                if best is None or size < best[0]:
                    best = (size, rows_on_i, p_r, p_c)
    assert best is not None, shape
    _, rows_on_i, p_r, p_c = best

    def index(i, j):
        gr, gc = (i, j) if rows_on_i else (j, i)
        return jnp.minimum(gr, p_r - 1), jnp.minimum(gc, p_c - 1)

    block = (r // p_r, c // p_c)
    return (pl.BlockSpec((None,) + block, lambda i, j: (layer,) + index(i, j)), pl.BlockSpec(block, index))


def _ffn(x, norm_g, weights, layer, ride, *, name):
    t, d = x.shape
    f = weights[-1].shape[0]
    tm = _pick(FFN_ROWS, t)
    tf = _pick(FFN_COLS, f)
    n_i, n_f = t // tm, f // tf
    ride_specs = [_ride_specs(a.shape[1:], l, n_i, n_f) for a, l in ride]
    out = pl.pallas_call(
        functools.partial(_ffn_kernel, n_ride=len(ride)),
        grid=(n_i, n_f),
        in_specs=[pl.BlockSpec((tm, d), lambda i, j: (i, 0)),
                  pl.BlockSpec((None, 1, d), lambda i, j: (layer, 0, 0)),
                  pl.BlockSpec((d, tf), lambda i, j: (0, j)),
                  pl.BlockSpec((d, tf), lambda i, j: (0, j)),
                  pl.BlockSpec((tf, d), lambda i, j: (j, 0))] + [s[0] for s in ride_specs],
        out_specs=[pl.BlockSpec((tm, d), lambda i, j: (i, 0))] + [s[1] for s in ride_specs],
        out_shape=[jax.ShapeDtypeStruct((t, d), F32)] + [jax.ShapeDtypeStruct(a.shape[1:], BF16) for a, _ in ride],
        scratch_shapes=[pltpu.VMEM((tm, d), BF16)],
        compiler_params=_compiler_params("arbitrary", "arbitrary"),
        name=name,
    )(x, norm_g, *weights, *(a for a, _ in ride))
    return out[0], tuple(out[1:])


def _inproj_kernel(h_ref, g_ref, w_ref, z_ref, xn_ref, *, gelu_tiles, linear_tiles):
    j = pl.program_id(1)

    @pl.when(j == 0)
    def _():
        def init(rows):
            xn_ref[rows, :] = _rmsnorm(h_ref[rows, :], g_ref[...]).astype(BF16)
        _for_rows(h_ref.shape[0], NORM_ROWS, init)

    def project(act):
        z = jnp.dot(xn_ref[...], w_ref[...], preferred_element_type=F32)
        z_ref[...] = act(z).astype(z_ref.dtype)

    @pl.when(j < gelu_tiles)
    def _():
        project(_gelu)

    @pl.when((j >= gelu_tiles) & (j < gelu_tiles + linear_tiles))
    def _():
        project(lambda z: z)

    @pl.when(j >= gelu_tiles + linear_tiles)
    def _():
        project(_sigmoid)


def _inproj(h, norm_g, w_in, layer, n_gelu, n_linear, *, name):
    t, d = h.shape
    n = w_in.shape[-1]
    tm = _pick(INPROJ_ROWS, t)
    tn = _pick(INPROJ_COLS, n)
    assert n_gelu % tn == 0 and n_linear % tn == 0
    return pl.pallas_call(
        functools.partial(_inproj_kernel, gelu_tiles=n_gelu // tn, linear_tiles=n_linear // tn),
        grid=(t // tm, n // tn),
        in_specs=[
            pl.BlockSpec((tm, d), lambda i, j: (i, 0)),
            pl.BlockSpec((None, 1, d), lambda i, j: (layer, 0, 0)),
            pl.BlockSpec((d, tn), lambda i, j: (0, j)),
        ],
        out_specs=pl.BlockSpec((tm, tn), lambda i, j: (i, j)),
        out_shape=jax.ShapeDtypeStruct((t, n), BF16),
        scratch_shapes=[pltpu.VMEM((tm, d), BF16)],
        compiler_params=_compiler_params("parallel", "arbitrary"),
        name=name,
    )(h, norm_g, w_in)


def _mixer_kernel(z_ref, h_ref, lng_ref, lnb_ref, sguw_ref, sgub_ref, convw_ref, poolw_ref, pscale_ref,
                  wa_ref, wb_ref, wc_ref, wout_ref, o_ref, ybuf_ref, xbuf_ref, *, tiles_per_seq, d_a, d_b, d_c):
    tm, d = h_ref.shape
    n_groups_a = sguw_ref.shape[0]
    dg_a = d_a // n_groups_a
    dg_c = d_c // len(POOL_WINDOWS)
    o1 = 2 * d_a
    o2 = o1 + 3 * d_b
    o3 = o2 + d_c
    tile_in_seq = pl.program_id(0) % tiles_per_seq

    @pl.when(tile_in_seq == 0)
    def _():
        ybuf_ref[0:HALO, :] = jnp.zeros((HALO, d_b), F32)
        xbuf_ref[0:HALO, :] = jnp.zeros((HALO, d_c), F32)

    v = z_ref[:, d_a:o1].astype(F32)
    mu = jnp.mean(v, axis=-1, keepdims=True)
    vc = v - mu
    var = jnp.mean(vc * vc, axis=-1, keepdims=True)
    vn = (vc * lax.rsqrt(var + EPS) * lng_ref[...] + lnb_ref[...]).astype(BF16)
    blocks = []
    for blk in range(tm // SGU_BLOCK):
        rows = slice(blk * SGU_BLOCK, (blk + 1) * SGU_BLOCK)
        groups = [jnp.dot(sguw_ref[g], vn[rows, g * dg_a:(g + 1) * dg_a], preferred_element_type=F32)
                  for g in range(n_groups_a)]
        blocks.append(jnp.concatenate(groups, axis=1) + sgub_ref[...])
    vm = jnp.concatenate(blocks, axis=0) if len(blocks) > 1 else blocks[0]
    ya = (z_ref[:, 0:d_a].astype(F32) * vm).astype(BF16)

    ybuf_ref[HALO:HALO + tm, :] = (z_ref[:, o1 + d_b:o1 + 2 * d_b].astype(F32)
                                   * z_ref[:, o1 + 2 * d_b:o2].astype(F32))
    conv = convw_ref[CONV_W - 1:CONV_W, :] * ybuf_ref[HALO:HALO + tm, :]
    for k in range(CONV_W - 1):
        back = CONV_W - 1 - k
        conv += convw_ref[k:k + 1, :] * ybuf_ref[HALO - back:HALO - back + tm, :]
    yb = (z_ref[:, o1:o1 + d_b].astype(F32) * conv).astype(BF16)
    ybuf_ref[0:HALO, :] = ybuf_ref[tm:tm + HALO, :]

    xbuf_ref[HALO:HALO + tm, :] = z_ref[:, o2:o3].astype(F32)
    frames = (tile_in_seq * tm + 1 + lax.broadcasted_iota(jnp.int32, (tm, 1), 0)).astype(F32)
    yc_groups = []
    for g, w in enumerate(POOL_WINDOWS):
        cols = slice(g * dg_c, (g + 1) * dg_c)
        cur = xbuf_ref[HALO:HALO + tm, cols]
        total = cur
        for back in range(1, w):
            total += xbuf_ref[HALO - back:HALO - back + tm, cols]
        pooled = (total / jnp.minimum(frames, F32(w)) - cur).astype(BF16)
        yc_groups.append(jnp.dot(pooled, poolw_ref[g], preferred_element_type=F32))
    yc = (jnp.concatenate(yc_groups, axis=1) * pscale_ref[...]).astype(BF16)
    xbuf_ref[0:HALO, :] = xbuf_ref[tm:tm + HALO, :]

    m = z_ref[:, o3:o3 + d].astype(F32) * jnp.dot(ya, wa_ref[...], preferred_element_type=F32)
    m += z_ref[:, o3 + d:o3 + 2 * d].astype(F32) * jnp.dot(yb, wb_ref[...], preferred_element_type=F32)
    m += z_ref[:, o3 + 2 * d:o3 + 3 * d].astype(F32) * jnp.dot(yc, wc_ref[...], preferred_element_type=F32)
    o_ref[...] = h_ref[...] + jnp.dot(m.astype(BF16), wout_ref[...], preferred_element_type=F32)


def _resident(shape, layer):
    zeros = (0,) * len(shape)
    return pl.BlockSpec((None,) + tuple(shape), lambda i: (layer,) + zeros, pipeline_mode=pl.Buffered(1))


def _mixer(z, h, seq, ln_g, ln_b, sgu_w, sgu_bias, conv_w, pool_w, pool_scale, w_a, w_b, w_c, w_out, layer,
           *, name):
    t, d = h.shape
    n_in = z.shape[-1]
    d_a, d_b, d_c = w_a.shape[1], w_b.shape[1], w_c.shape[1]
    tm = _pick(MIXER_ROWS, seq)
    assert tm % SGU_BLOCK == 0 and tm >= HALO
    params = (ln_g, ln_b, sgu_w, sgu_bias, conv_w, pool_w, pool_scale, w_a, w_b, w_c, w_out)
    return pl.pallas_call(
        functools.partial(_mixer_kernel, tiles_per_seq=seq // tm, d_a=d_a, d_b=d_b, d_c=d_c),
        grid=(t // tm,),
        in_specs=[pl.BlockSpec((tm, n_in), lambda i: (i, 0)), pl.BlockSpec((tm, d), lambda i: (i, 0))]
                 + [_resident(a.shape[1:], layer) for a in params],
        out_specs=pl.BlockSpec((tm, d), lambda i: (i, 0)),
        out_shape=jax.ShapeDtypeStruct((t, d), F32),
        scratch_shapes=[pltpu.VMEM((HALO + tm, d_b), F32), pltpu.VMEM((HALO + tm, d_c), F32)],
        compiler_params=_compiler_params("arbitrary"),
        name=name,
    )(z, h, *params)


def _ple_kernel(h_ref, p_ref, g_ref, wg_ref, wp_ref, fg_ref, o_ref, *, final_norm):
    h = h_ref[...]
    gate = jnp.dot(_rmsnorm(h, g_ref[...]).astype(BF16), wg_ref[...], preferred_element_type=F32)
    proj = jnp.dot(p_ref[...].astype(BF16), wp_ref[...], preferred_element_type=F32)
    out = h + _sigmoid(gate) * proj
    if final_norm:
        out = _rmsnorm(out, fg_ref[...])
    o_ref[...] = out


def _ple(h, p, norm_g, w_gate, w_proj, final_g, layer, final_norm, *, name):
    t, d = h.shape
    d_ple = p.shape[-1]
    tm = _pick(PLE_ROWS, t)
    tiles = t // tm
    return pl.pallas_call(
        functools.partial(_ple_kernel, final_norm=final_norm),
        grid=(tiles,),
        in_specs=[
            pl.BlockSpec((tm, d), lambda i: (i, 0)),
            pl.BlockSpec((tm, d_ple), lambda i: (layer * tiles + i, 0)),
            _resident((1, d), layer),
            _resident((d, d), layer),
            _resident((d_ple, d), layer),
            pl.BlockSpec((1, d), lambda i: (0, 0)),
        ],
        out_specs=pl.BlockSpec((tm, d), lambda i: (i, 0)),
        out_shape=jax.ShapeDtypeStruct((t, d), F32),
        compiler_params=_compiler_params("parallel"),
        name=name,
    )(h, p, norm_g, w_gate, w_proj, final_g)


def kernel(x, p, ffn1_norm, ffn1_w_gate, ffn1_w_up, ffn1_w_down, mix_norm, w_in, sgu_norm_g, sgu_norm_b, sgu_w, sgu_b, conv_w, pool_w, pool_scale, w_branch_a, w_branch_b, w_branch_c, w_out, ffn2_norm, ffn2_w_gate, ffn2_w_up, ffn2_w_down, ple_norm, ple_w_gate, ple_w_proj, final_norm):
    bsz, seq, d = x.shape
    depth = p.shape[0]
    t = bsz * seq
    d_a = w_branch_a.shape[1]
    d_b = w_branch_b.shape[1]
    d_c = w_branch_c.shape[1]
    n_groups_a = sgu_w.shape[1]

    bf = lambda a: a.astype(BF16)
    row = lambda a: a[:, None, :]

    pos = jnp.arange(SGU_BLOCK)
    mask = (pos[None, :] // CHUNK) <= (pos[:, None] // CHUNK)
    sgu_w_masked = bf(jnp.where(mask[None, None], sgu_w, jnp.zeros_like(sgu_w)))
    sgu_bias = jnp.repeat(jnp.swapaxes(sgu_b, 1, 2), d_a // n_groups_a, axis=2)

    mixer_params = (row(sgu_norm_g), row(sgu_norm_b), sgu_w_masked, sgu_bias, conv_w, bf(pool_w),
                    row(pool_scale), bf(w_branch_a), bf(w_branch_b), bf(w_branch_c), bf(w_out))
    ple_g, ple_wg, ple_wp = row(ple_norm), bf(ple_w_gate), bf(ple_w_proj)
    final_g = final_norm[None, :]
    ffn1_f32 = (ffn1_w_gate, ffn1_w_up, ffn1_w_down)
    ffn2_f32 = (ffn2_w_gate, ffn2_w_up, ffn2_w_down)

    h = x.reshape(t, d)
    p2 = p.reshape(depth * t, p.shape[-1])
    w1 = tuple(bf(w[0]) for w in ffn1_f32)
    w_in_l = bf(w_in[0])
    for layer in range(depth):
        more = layer + 1 < depth
        ride = [(w, layer) for w in ffn2_f32] + ([(w_in, layer + 1)] if more else [])
        h, cast = _ffn(h, row(ffn1_norm), w1, layer, ride, name=f"ffn1_L{layer}")
        w2, w_in_next = cast[:3], cast[3:]
        z = _inproj(h, row(mix_norm), w_in_l, layer, 2 * d_a, 3 * d_b + d_c, name=f"inproj_L{layer}")
        h = _mixer(z, h, seq, *mixer_params, layer, name=f"mixer_L{layer}")
        h, w1 = _ffn(h, row(ffn2_norm), w2, layer, [(w, layer + 1) for w in ffn1_f32] if more else [],
                     name=f"ffn2_L{layer}")
        if more:
            w_in_l = w_in_next[0]
        h = _ple(h, p2, ple_g, ple_wg, ple_wp, final_g, layer, layer == depth - 1, name=f"ple_L{layer}")
    return h.reshape(bsz, seq, d)
```

```python
import functools

import jax
import jax.numpy as jnp
from jax import lax
from jax.experimental import pallas as pl
from jax.experimental.pallas import tpu as pltpu

EPS = 1e-6
CHUNK = 64
SGU_BLOCK = 128
CONV_W = 3
POOL_WINDOWS = (2, 4, 8, 16)
HALO = 16
POOL_HISTORY = 128

V7X_VMEM_BYTES = 64 * 1024 * 1024
VMEM_LIMIT_BYTES = V7X_VMEM_BYTES - 4 * 1024 * 1024
BF16_SUBLANES, LANES = 16, 128

FFN_ROWS, FFN_COLS = 1024, 512
INPROJ_ROWS, INPROJ_COLS = 1024, 2048
MIXER_ROWS = 256
PLE_ROWS = 512
NORM_ROWS = 256

F32 = jnp.float32
BF16 = jnp.bfloat16


def _pick(pref, n):
    t = min(pref, n)
    assert n % t == 0, (pref, n)
    return t


def _rmsnorm(x, g):
    return x * lax.rsqrt(jnp.mean(x * x, axis=-1, keepdims=True) + EPS) * g


def _gelu(x):
    return 0.5 * x * (1.0 + lax.erf(x * (2.0 ** -0.5)))


def _sigmoid(x):
    return 0.5 + 0.5 * jnp.tanh(0.5 * x)


def _for_rows(total, chunk, body):
    def step(r, carry):
        body(pl.ds(pl.multiple_of(r * chunk, chunk), chunk))
        return carry

    lax.fori_loop(0, total // chunk, step, 0)


def _compiler_params(*semantics):
    return pltpu.CompilerParams(dimension_semantics=semantics, vmem_limit_bytes=VMEM_LIMIT_BYTES)


def _ffn_kernel(*refs, n_ride):
    x_ref, g_ref, wg_ref, wu_ref, wd_ref = refs[:5]
    ride_in = refs[5:5 + n_ride]
    o_ref = refs[5 + n_ride]
    ride_out = refs[6 + n_ride:6 + 2 * n_ride]
    xn_ref = refs[-1]

    for src, dst in zip(ride_in, ride_out):
        dst[...] = src[...].astype(BF16)

    @pl.when(pl.program_id(1) == 0)
    def _():
        def init(rows):
            x = x_ref[rows, :]
            xn_ref[rows, :] = _rmsnorm(x, g_ref[...]).astype(BF16)
            o_ref[rows, :] = x
        _for_rows(x_ref.shape[0], NORM_ROWS, init)

    xn = xn_ref[...]
    gate = jnp.dot(xn, wg_ref[...], preferred_element_type=F32)
    up = jnp.dot(xn, wu_ref[...], preferred_element_type=F32)
    half = 0.5 * gate
    act = (0.5 * (half + half * jnp.tanh(half)) * up).astype(BF16)
    o_ref[...] += jnp.dot(act, wd_ref[...], preferred_element_type=F32)


def _ride_specs(shape, layer, n_i, n_j):
    r, c = shape
    best = None
    for rows_on_i in (True, False):
        n_r, n_c = (n_i, n_j) if rows_on_i else (n_j, n_i)
        for p_r in (n_r, n_r - 1, 1):
            for p_c in (n_c, n_c - 1, 1):
                if p_r < 1 or p_c < 1 or r % p_r or c % p_c or (r // p_r) % BF16_SUBLANES or (c // p_c) % LANES:
                    continue
                if (p_r not in (n_r, 1) and n_r != n_j) or (p_c not in (n_c, 1) and n_c != n_j):
                    continue
                size = (r // p_r) * (c // p_c)
                if best is None or size < best[0]:
                    best = (size, rows_on_i, p_r, p_c)
    assert best is not None, shape
    _, rows_on_i, p_r, p_c = best

    def index(i, j):
        gr, gc = (i, j) if rows_on_i else (j, i)
        return jnp.minimum(gr, p_r - 1), jnp.minimum(gc, p_c - 1)

    block = (r // p_r, c // p_c)
    return (pl.BlockSpec((None,) + block, lambda i, j: (layer,) + index(i, j)), pl.BlockSpec(block, index))


def _ffn(x, norm_g, weights, layer, ride, *, name):
    t, d = x.shape
    f = weights[-1].shape[0]
    tm = _pick(FFN_ROWS, t)
    tf = _pick(FFN_COLS, f)
    n_i, n_f = t // tm, f // tf
    ride_specs = [_ride_specs(a.shape[1:], l, n_i, n_f) for a, l in ride]
    out = pl.pallas_call(
        functools.partial(_ffn_kernel, n_ride=len(ride)),
        grid=(n_i, n_f),
        in_specs=[pl.BlockSpec((tm, d), lambda i, j: (i, 0)),
                  pl.BlockSpec((None, 1, d), lambda i, j: (layer, 0, 0)),
                  pl.BlockSpec((d, tf), lambda i, j: (0, j)),
                  pl.BlockSpec((d, tf), lambda i, j: (0, j)),
                  pl.BlockSpec((tf, d), lambda i, j: (j, 0))] + [s[0] for s in ride_specs],
        out_specs=[pl.BlockSpec((tm, d), lambda i, j: (i, 0))] + [s[1] for s in ride_specs],
        out_shape=[jax.ShapeDtypeStruct((t, d), F32)] + [jax.ShapeDtypeStruct(a.shape[1:], BF16) for a, _ in ride],
        scratch_shapes=[pltpu.VMEM((tm, d), BF16)],
        compiler_params=_compiler_params("arbitrary", "arbitrary"),
        name=name,
    )(x, norm_g, *weights, *(a for a, _ in ride))
    return out[0], tuple(out[1:])


def _inproj_kernel(h_ref, g_ref, w_ref, z_ref, xn_ref, *, gelu_tiles, linear_tiles):
    j = pl.program_id(1)

    @pl.when(j == 0)
    def _():
        def init(rows):
            xn_ref[rows, :] = _rmsnorm(h_ref[rows, :], g_ref[...]).astype(BF16)
        _for_rows(h_ref.shape[0], NORM_ROWS, init)

    def project(act):
        z = jnp.dot(xn_ref[...], w_ref[...], preferred_element_type=F32)
        z_ref[...] = act(z).astype(z_ref.dtype)

    @pl.when(j < gelu_tiles)
    def _():
        project(_gelu)

    @pl.when((j >= gelu_tiles) & (j < gelu_tiles + linear_tiles))
    def _():
        project(lambda z: z)

    @pl.when(j >= gelu_tiles + linear_tiles)
    def _():
        project(_sigmoid)


def _inproj(h, norm_g, w_in, layer, n_gelu, n_linear, *, name):
    t, d = h.shape
    n = w_in.shape[-1]
    tm = _pick(INPROJ_ROWS, t)
    tn = _pick(INPROJ_COLS, n)
    assert n_gelu % tn == 0 and n_linear % tn == 0
    return pl.pallas_call(
        functools.partial(_inproj_kernel, gelu_tiles=n_gelu // tn, linear_tiles=n_linear // tn),
        grid=(t // tm, n // tn),
        in_specs=[
            pl.BlockSpec((tm, d), lambda i, j: (i, 0)),
            pl.BlockSpec((None, 1, d), lambda i, j: (layer, 0, 0)),
            pl.BlockSpec((d, tn), lambda i, j: (0, j)),
        ],
        out_specs=pl.BlockSpec((tm, tn), lambda i, j: (i, j)),
        out_shape=jax.ShapeDtypeStruct((t, n), BF16),
        scratch_shapes=[pltpu.VMEM((tm, d), BF16)],
        compiler_params=_compiler_params("parallel", "arbitrary"),
        name=name,
    )(h, norm_g, w_in)


def _mixer_kernel(z_ref, h_ref, lng_ref, lnb_ref, sguw_ref, sgub_ref, convw_ref, poolw_ref, pscale_ref,
                  wa_ref, wb_ref, wc_ref, wout_ref, o_ref, ybuf_ref, xhist_ref, band_ref, *, tiles_per_seq, d_a, d_b,
                  d_c):
    tm, d = h_ref.shape
    n_groups_a = sguw_ref.shape[0]
    dg_a = d_a // n_groups_a
    dg_c = d_c // len(POOL_WINDOWS)
    o1 = 2 * d_a
    o2 = o1 + 3 * d_b
    o3 = o2 + d_c
    tile_in_seq = pl.program_id(0) % tiles_per_seq

    @pl.when(tile_in_seq == 0)
    def _():
        ybuf_ref[0:HALO, :] = jnp.zeros((HALO, d_b), F32)
        xhist_ref[0:POOL_HISTORY, :] = jnp.zeros((POOL_HISTORY, d_c), BF16)

    @pl.when(pl.program_id(0) == 0)
    def _():
        t_idx = lax.broadcasted_iota(jnp.int32, (tm, POOL_HISTORY + tm), 0) + POOL_HISTORY
        lag = t_idx - lax.broadcasted_iota(jnp.int32, (tm, POOL_HISTORY + tm), 1)
        for g, w in enumerate(POOL_WINDOWS):
            band_ref[g] = jnp.where((lag >= 0) & (lag < w), 1.0, 0.0).astype(BF16)

    xhist_ref[POOL_HISTORY:POOL_HISTORY + tm, :] = z_ref[:, o2:o3]
    totals = [jnp.dot(band_ref[g], xhist_ref[:, g * dg_c:(g + 1) * dg_c], preferred_element_type=F32)
              for g in range(len(POOL_WINDOWS))]
    xhist_ref[0:POOL_HISTORY, :] = xhist_ref[tm:tm + POOL_HISTORY, :]

    v = z_ref[:, d_a:o1].astype(F32)
    mu = jnp.mean(v, axis=-1, keepdims=True)
    vc = v - mu
    var = jnp.mean(vc * vc, axis=-1, keepdims=True)
    vn = (vc * lax.rsqrt(var + EPS) * lng_ref[...] + lnb_ref[...]).astype(BF16)
    blocks = []
    for blk in range(tm // SGU_BLOCK):
        rows = slice(blk * SGU_BLOCK, (blk + 1) * SGU_BLOCK)
        groups = [jnp.dot(sguw_ref[g], vn[rows, g * dg_a:(g + 1) * dg_a], preferred_element_type=F32)
                  for g in range(n_groups_a)]
        blocks.append(jnp.concatenate(groups, axis=1) + sgub_ref[...])
    vm = jnp.concatenate(blocks, axis=0) if len(blocks) > 1 else blocks[0]
    ya = (z_ref[:, 0:d_a].astype(F32) * vm).astype(BF16)
    m = z_ref[:, o3:o3 + d].astype(F32) * jnp.dot(ya, wa_ref[...], preferred_element_type=F32)

    ybuf_ref[HALO:HALO + tm, :] = (z_ref[:, o1 + d_b:o1 + 2 * d_b].astype(F32)
                                   * z_ref[:, o1 + 2 * d_b:o2].astype(F32))
    conv = convw_ref[CONV_W - 1:CONV_W, :] * ybuf_ref[HALO:HALO + tm, :]
    for k in range(CONV_W - 1):
        back = CONV_W - 1 - k
        conv += convw_ref[k:k + 1, :] * ybuf_ref[HALO - back:HALO - back + tm, :]
    yb = (z_ref[:, o1:o1 + d_b].astype(F32) * conv).astype(BF16)
    ybuf_ref[0:HALO, :] = ybuf_ref[tm:tm + HALO, :]
    m += z_ref[:, o3 + d:o3 + 2 * d].astype(F32) * jnp.dot(yb, wb_ref[...], preferred_element_type=F32)

    frames = (tile_in_seq * tm + 1 + lax.broadcasted_iota(jnp.int32, (tm, 1), 0)).astype(F32)
    yc_groups = []
    for g, w in enumerate(POOL_WINDOWS):
        cols = slice(g * dg_c, (g + 1) * dg_c)
        pooled = (totals[g] * (1.0 / jnp.minimum(frames, F32(w))) - z_ref[:, o2 + g * dg_c:o2 + (g + 1) * dg_c]
                  .astype(F32)).astype(BF16)
        yc_groups.append(jnp.dot(pooled, poolw_ref[g], preferred_element_type=F32))
    yc = (jnp.concatenate(yc_groups, axis=1) * pscale_ref[...]).astype(BF16)
    m += z_ref[:, o3 + 2 * d:o3 + 3 * d].astype(F32) * jnp.dot(yc, wc_ref[...], preferred_element_type=F32)

    o_ref[...] = h_ref[...] + jnp.dot(m.astype(BF16), wout_ref[...], preferred_element_type=F32)


def _resident(a, layer, ndim):
    zeros = (0,) * ndim
    if a.ndim == ndim:
        return pl.BlockSpec(a.shape, lambda i: zeros, pipeline_mode=pl.Buffered(1))
    return pl.BlockSpec((None,) + a.shape[1:], lambda i: (layer,) + zeros, pipeline_mode=pl.Buffered(1))


def _mixer(z, h, seq, ln_g, ln_b, sgu_w, sgu_bias, conv_w, pool_w, pool_scale, w_a, w_b, w_c, w_out, layer,
           *, name):
    t, d = h.shape
    n_in = z.shape[-1]
    d_a, d_b, d_c = w_a.shape[-2], w_b.shape[-2], w_c.shape[-2]
    tm = _pick(MIXER_ROWS, seq)
    assert tm % SGU_BLOCK == 0 and tm >= POOL_HISTORY >= max(POOL_WINDOWS) - 1 and HALO >= CONV_W - 1
    params = (ln_g, ln_b, sgu_w, sgu_bias, conv_w, pool_w, pool_scale, w_a, w_b, w_c, w_out)
    ndims = (2, 2, 3, 2, 2, 3, 2, 2, 2, 2, 2)
    return pl.pallas_call(
        functools.partial(_mixer_kernel, tiles_per_seq=seq // tm, d_a=d_a, d_b=d_b, d_c=d_c),
        grid=(t // tm,),
        in_specs=[pl.BlockSpec((tm, n_in), lambda i: (i, 0)), pl.BlockSpec((tm, d), lambda i: (i, 0))]
                 + [_resident(a, layer, n) for a, n in zip(params, ndims)],
        out_specs=pl.BlockSpec((tm, d), lambda i: (i, 0)),
        out_shape=jax.ShapeDtypeStruct((t, d), F32),
        scratch_shapes=[pltpu.VMEM((HALO + tm, d_b), F32), pltpu.VMEM((POOL_HISTORY + tm, d_c), BF16),
                        pltpu.VMEM((len(POOL_WINDOWS), tm, POOL_HISTORY + tm), BF16)],
        compiler_params=_compiler_params("arbitrary"),
        name=name,
    )(z, h, *params)


def _ple_kernel(h_ref, p_ref, g_ref, wg_ref, wp_ref, fg_ref, o_ref, *, final_norm):
    h = h_ref[...]
    proj = jnp.dot(p_ref[...].astype(BF16), wp_ref[...], preferred_element_type=F32)
    gate = jnp.dot(_rmsnorm(h, g_ref[...]).astype(BF16), wg_ref[...], preferred_element_type=F32)
    out = h + _sigmoid(gate) * proj
    if final_norm:
        out = _rmsnorm(out, fg_ref[...])
    o_ref[...] = out


def _ple(h, p, norm_g, w_gate, w_proj, final_g, layer, final_norm, *, name):
    t, d = h.shape
    d_ple = p.shape[-1]
    tm = _pick(PLE_ROWS, t)
    tiles = t // tm
    return pl.pallas_call(
        functools.partial(_ple_kernel, final_norm=final_norm),
        grid=(tiles,),
        in_specs=[
            pl.BlockSpec((tm, d), lambda i: (i, 0)),
            pl.BlockSpec((tm, d_ple), lambda i: (layer * tiles + i, 0)),
            _resident(norm_g, layer, 2),
            _resident(w_gate, layer, 2),
            _resident(w_proj, layer, 2),
            pl.BlockSpec((1, d), lambda i: (0, 0)),
        ],
        out_specs=pl.BlockSpec((tm, d), lambda i: (i, 0)),
        out_shape=jax.ShapeDtypeStruct((t, d), F32),
        compiler_params=_compiler_params("parallel"),
        name=name,
    )(h, p, norm_g, w_gate, w_proj, final_g)


def kernel(x, p, ffn1_norm, ffn1_w_gate, ffn1_w_up, ffn1_w_down, mix_norm, w_in, sgu_norm_g, sgu_norm_b, sgu_w, sgu_b, conv_w, pool_w, pool_scale, w_branch_a, w_branch_b, w_branch_c, w_out, ffn2_norm, ffn2_w_gate, ffn2_w_up, ffn2_w_down, ple_norm, ple_w_gate, ple_w_proj, final_norm):
    bsz, seq, d = x.shape
    depth = p.shape[0]
    t = bsz * seq
    d_a = w_branch_a.shape[1]
    d_b = w_branch_b.shape[1]
    d_c = w_branch_c.shape[1]
    n_groups_a = sgu_w.shape[1]

    bf = lambda a: a.astype(BF16)
    row = lambda a: a[:, None, :]

    pos = jnp.arange(SGU_BLOCK)
    mask = (pos[None, :] // CHUNK) <= (pos[:, None] // CHUNK)
    sgu_w_masked = bf(jnp.where(mask[None, None], sgu_w, jnp.zeros_like(sgu_w)))
    sgu_bias = jnp.repeat(jnp.swapaxes(sgu_b, 1, 2), d_a // n_groups_a, axis=2)

    mixer_small = (row(sgu_norm_g), row(sgu_norm_b), sgu_w_masked, sgu_bias, conv_w, bf(pool_w), row(pool_scale))
    w_a, w_b, w_c, w_o = bf(w_branch_a), bf(w_branch_b), bf(w_branch_c), bf(w_out)
    ple_g, ple_wg, ple_wp = row(ple_norm), bf(ple_w_gate), bf(ple_w_proj)
    final_g = final_norm[None, :]
    ffn1_f32 = (ffn1_w_gate, ffn1_w_up, ffn1_w_down)
    ffn2_f32 = (ffn2_w_gate, ffn2_w_up, ffn2_w_down)

    h = x.reshape(t, d)
    p2 = p.reshape(depth * t, p.shape[-1])
    w1 = tuple(bf(w[0]) for w in ffn1_f32)
    w_in_l = bf(w_in[0])
    for layer in range(depth):
        more = layer + 1 < depth
        ride = [(w, layer) for w in ffn2_f32] + ([(w_in, layer + 1)] if more else [])
        h, cast = _ffn(h, row(ffn1_norm), w1, layer, ride, name=f"ffn1_L{layer}")
        w2, w_in_next = cast[:3], cast[3:]
        z = _inproj(h, row(mix_norm), w_in_l, layer, 2 * d_a, 3 * d_b + d_c, name=f"inproj_L{layer}")
        h = _mixer(z, h, seq, *mixer_small, w_a, w_b, w_c, w_o, layer, name=f"mixer_L{layer}")
        h, w1 = _ffn(h, row(ffn2_norm), w2, layer, [(w, layer + 1) for w in ffn1_f32] if more else [],
                     name=f"ffn2_L{layer}")
        if more:
            w_in_l = w_in_next[0]
        h = _ple(h, p2, ple_g, ple_wg, ple_wp, final_g, layer, layer == depth - 1, name=f"ple_L{layer}")
    return h.reshape(bsz, seq, d)
```

```python
import functools

import jax
import jax.numpy as jnp
from jax import lax
from jax.experimental import pallas as pl
from jax.experimental.pallas import tpu as pltpu

EPS = 1e-6
CHUNK = 64
SGU_BLOCK = 128
CONV_W = 3
POOL_WINDOWS = (2, 4, 8, 16)
HALO = 16
POOL_HISTORY = 128

V7X_VMEM_BYTES = 64 * 1024 * 1024
VMEM_LIMIT_BYTES = V7X_VMEM_BYTES - 4 * 1024 * 1024
BF16_SUBLANES, LANES = 16, 128

FFN_ROWS, FFN_COLS = 1024, 512
INPROJ_ROWS, INPROJ_COLS = 1024, 2048
MIXER_ROWS = 256
PLE_ROWS = 512
NORM_ROWS = 256

F32 = jnp.float32
BF16 = jnp.bfloat16


def _pick(pref, n):
    t = min(pref, n)
    assert n % t == 0, (pref, n)
    return t


def _rmsnorm(x, g):
    return x * lax.rsqrt(jnp.mean(x * x, axis=-1, keepdims=True) + EPS) * g


def _gelu(x):
    return 0.5 * x * (1.0 + lax.erf(x * (2.0 ** -0.5)))


def _sigmoid(x):
    return 0.5 + 0.5 * jnp.tanh(0.5 * x)


def _for_rows(total, chunk, body):
    def step(r, carry):
        body(pl.ds(pl.multiple_of(r * chunk, chunk), chunk))
        return carry

    lax.fori_loop(0, total // chunk, step, 0)


def _compiler_params(*semantics):
    return pltpu.CompilerParams(dimension_semantics=semantics, vmem_limit_bytes=VMEM_LIMIT_BYTES)


def _ffn_kernel(*refs, n_ride):
    x_ref, g_ref, wg_ref, wu_ref, wd_ref = refs[:5]
    ride_in = refs[5:5 + n_ride]
    o_ref = refs[5 + n_ride]
    ride_out = refs[6 + n_ride:6 + 2 * n_ride]
    xn_ref = refs[-1]

    for src, dst in zip(ride_in, ride_out):
        dst[...] = src[...].astype(BF16)

    @pl.when(pl.program_id(1) == 0)
    def _():
        def init(rows):
            x = x_ref[rows, :]
            xn_ref[rows, :] = _rmsnorm(x, g_ref[...]).astype(BF16)
            o_ref[rows, :] = x
        _for_rows(x_ref.shape[0], NORM_ROWS, init)

    xn = xn_ref[...]
    gate = jnp.dot(xn, wg_ref[...], preferred_element_type=F32)
    up = jnp.dot(xn, wu_ref[...], preferred_element_type=F32)
    half = 0.5 * gate
    act = (0.5 * (half + half * jnp.tanh(half)) * up).astype(BF16)
    o_ref[...] += jnp.dot(act, wd_ref[...], preferred_element_type=F32)


def _ride_specs(shape, layer, n_i, n_j):
    r, c = shape
    best = None
    for rows_on_i in (True, False):
        n_r, n_c = (n_i, n_j) if rows_on_i else (n_j, n_i)
        for p_r in (n_r, n_r - 1, 1):
            for p_c in (n_c, n_c - 1, 1):
                if p_r < 1 or p_c < 1 or r % p_r or c % p_c or (r // p_r) % BF16_SUBLANES or (c // p_c) % LANES:
                    continue
                if (p_r not in (n_r, 1) and n_r != n_j) or (p_c not in (n_c, 1) and n_c != n_j):
                    continue
                size = (r // p_r) * (c // p_c)
                if best is None or size < best[0]:
                    best = (size, rows_on_i, p_r, p_c)
    assert best is not None, shape
    _, rows_on_i, p_r, p_c = best

    def index(i, j):
        gr, gc = (i, j) if rows_on_i else (j, i)
        return jnp.minimum(gr, p_r - 1), jnp.minimum(gc, p_c - 1)

    block = (r // p_r, c // p_c)
    return (pl.BlockSpec((None,) + block, lambda i, j: (layer,) + index(i, j)), pl.BlockSpec(block, index))


def _ffn(x, norm_g, weights, layer, ride, *, name):
    t, d = x.shape
    f = weights[-1].shape[0]
    tm = _pick(FFN_ROWS, t)
    tf = _pick(FFN_COLS, f)
    n_i, n_f = t // tm, f // tf
    ride_specs = [_ride_specs(a.shape[1:], l, n_i, n_f) for a, l in ride]
    out = pl.pallas_call(
        functools.partial(_ffn_kernel, n_ride=len(ride)),
        grid=(n_i, n_f),
        in_specs=[pl.BlockSpec((tm, d), lambda i, j: (i, 0)),
                  pl.BlockSpec((None, 1, d), lambda i, j: (layer, 0, 0)),
                  pl.BlockSpec((d, tf), lambda i, j: (0, j)),
                  pl.BlockSpec((d, tf), lambda i, j: (0, j)),
                  pl.BlockSpec((tf, d), lambda i, j: (j, 0))] + [s[0] for s in ride_specs],
        out_specs=[pl.BlockSpec((tm, d), lambda i, j: (i, 0))] + [s[1] for s in ride_specs],
        out_shape=[jax.ShapeDtypeStruct((t, d), F32)] + [jax.ShapeDtypeStruct(a.shape[1:], BF16) for a, _ in ride],
        scratch_shapes=[pltpu.VMEM((tm, d), BF16)],
        compiler_params=_compiler_params("arbitrary", "arbitrary"),
        name=name,
    )(x, norm_g, *weights, *(a for a, _ in ride))
    return out[0], tuple(out[1:])


def _inproj_kernel(*refs, tile_acts, n_ride):
    h_ref, g_ref, w_ref = refs[:3]
    ride_in = refs[3:3 + n_ride]
    z_ref = refs[3 + n_ride]
    ride_out = refs[4 + n_ride:4 + 2 * n_ride]
    xn_ref = refs[-1]
    j = pl.program_id(1)

    for src, dst in zip(ride_in, ride_out):
        dst[...] = src[...].astype(BF16)

    @pl.when(j == 0)
    def _():
        def init(rows):
            xn_ref[rows, :] = _rmsnorm(h_ref[rows, :], g_ref[...]).astype(BF16)
        _for_rows(h_ref.shape[0], NORM_ROWS, init)

    for lo, hi, act in tile_acts:
        @pl.when((j >= lo) & (j < hi))
        def _(act=act):
            z = jnp.dot(xn_ref[...], w_ref[...], preferred_element_type=F32)
            z_ref[...] = act(z).astype(z_ref.dtype)


def _inproj(h, norm_g, w_in, layer, n_gelu, n_linear, ride, *, name):
    t, d = h.shape
    n = w_in.shape[-1]
    tm = _pick(INPROJ_ROWS, t)
    tn = _pick(INPROJ_COLS, n)
    n_i, n_j = t // tm, n // tn
    assert n_gelu % tn == 0 and n_linear % tn == 0
    g_end, l_end = n_gelu // tn, (n_gelu + n_linear) // tn
    acts = [r for r in ((0, g_end, _gelu), (g_end, l_end, lambda z: z), (l_end, n_j, _sigmoid)) if r[0] < r[1]]
    ride_specs = [_ride_specs(a.shape[1:], l, n_i, n_j) for a, l in ride]
    out = pl.pallas_call(
        functools.partial(_inproj_kernel, tile_acts=tuple(acts), n_ride=len(ride)),
        grid=(n_i, n_j),
        in_specs=[
            pl.BlockSpec((tm, d), lambda i, j: (i, 0)),
            pl.BlockSpec((None, 1, d), lambda i, j: (layer, 0, 0)),
            pl.BlockSpec((d, tn), lambda i, j: (0, j)),
        ] + [s[0] for s in ride_specs],
        out_specs=[pl.BlockSpec((tm, tn), lambda i, j: (i, j))] + [s[1] for s in ride_specs],
        out_shape=[jax.ShapeDtypeStruct((t, n), BF16)] + [jax.ShapeDtypeStruct(a.shape[1:], BF16) for a, _ in ride],
        scratch_shapes=[pltpu.VMEM((tm, d), BF16)],
        compiler_params=_compiler_params("arbitrary", "arbitrary"),
        name=name,
    )(h, norm_g, w_in, *(a for a, _ in ride))
    return out[0], tuple(out[1:])


def _mixer_kernel(z_ref, h_ref, lng_ref, lnb_ref, sguw_ref, sgub_ref, convw_ref, poolw_ref, pscale_ref,
                  wa_ref, wb_ref, wc_ref, wout_ref, o_ref, ybuf_ref, xhist_ref, band_ref, *, tiles_per_seq, d_a, d_b,
                  d_c):
    tm, d = h_ref.shape
    n_groups_a = sguw_ref.shape[0]
    dg_a = d_a // n_groups_a
    dg_c = d_c // len(POOL_WINDOWS)
    o1 = 2 * d_a
    o2 = o1 + 3 * d_b
    o3 = o2 + d_c
    tile_in_seq = pl.program_id(0) % tiles_per_seq

    @pl.when(tile_in_seq == 0)
    def _():
        ybuf_ref[0:HALO, :] = jnp.zeros((HALO, d_b), F32)
        xhist_ref[0:POOL_HISTORY, :] = jnp.zeros((POOL_HISTORY, d_c), BF16)

    @pl.when(pl.program_id(0) == 0)
    def _():
        t_idx = lax.broadcasted_iota(jnp.int32, (tm, POOL_HISTORY + tm), 0) + POOL_HISTORY
        lag = t_idx - lax.broadcasted_iota(jnp.int32, (tm, POOL_HISTORY + tm), 1)
        for g, w in enumerate(POOL_WINDOWS):
            band_ref[g] = jnp.where((lag >= 0) & (lag < w), 1.0, 0.0).astype(BF16)

    xhist_ref[POOL_HISTORY:POOL_HISTORY + tm, :] = z_ref[:, o2:o3]
    totals = [jnp.dot(band_ref[g], xhist_ref[:, g * dg_c:(g + 1) * dg_c], preferred_element_type=F32)
              for g in range(len(POOL_WINDOWS))]
    xhist_ref[0:POOL_HISTORY, :] = xhist_ref[tm:tm + POOL_HISTORY, :]

    v = z_ref[:, d_a:o1].astype(F32)
    mu = jnp.mean(v, axis=-1, keepdims=True)
    vc = v - mu
    var = jnp.mean(vc * vc, axis=-1, keepdims=True)
    vn = (vc * lax.rsqrt(var + EPS) * lng_ref[...] + lnb_ref[...]).astype(BF16)
    blocks = []
    for blk in range(tm // SGU_BLOCK):
        rows = slice(blk * SGU_BLOCK, (blk + 1) * SGU_BLOCK)
        groups = [jnp.dot(sguw_ref[g], vn[rows, g * dg_a:(g + 1) * dg_a], preferred_element_type=F32)
                  for g in range(n_groups_a)]
        blocks.append(jnp.concatenate(groups, axis=1) + sgub_ref[...])
    vm = jnp.concatenate(blocks, axis=0) if len(blocks) > 1 else blocks[0]
    ya = (z_ref[:, 0:d_a].astype(F32) * vm).astype(BF16)
    m = z_ref[:, o3:o3 + d].astype(F32) * jnp.dot(ya, wa_ref[...], preferred_element_type=F32)

    ybuf_ref[HALO:HALO + tm, :] = (z_ref[:, o1 + d_b:o1 + 2 * d_b].astype(F32)
                                   * z_ref[:, o1 + 2 * d_b:o2].astype(F32))
    conv = convw_ref[CONV_W - 1:CONV_W, :] * ybuf_ref[HALO:HALO + tm, :]
    for k in range(CONV_W - 1):
        back = CONV_W - 1 - k
        conv += convw_ref[k:k + 1, :] * ybuf_ref[HALO - back:HALO - back + tm, :]
    yb = (z_ref[:, o1:o1 + d_b].astype(F32) * conv).astype(BF16)
    ybuf_ref[0:HALO, :] = ybuf_ref[tm:tm + HALO, :]
    m += z_ref[:, o3 + d:o3 + 2 * d].astype(F32) * jnp.dot(yb, wb_ref[...], preferred_element_type=F32)

    frames = (tile_in_seq * tm + 1 + lax.broadcasted_iota(jnp.int32, (tm, 1), 0)).astype(F32)
    yc_groups = []
    for g, w in enumerate(POOL_WINDOWS):
        cols = slice(g * dg_c, (g + 1) * dg_c)
        pooled = (totals[g] * (1.0 / jnp.minimum(frames, F32(w))) - z_ref[:, o2 + g * dg_c:o2 + (g + 1) * dg_c]
                  .astype(F32)).astype(BF16)
        yc_groups.append(jnp.dot(pooled, poolw_ref[g], preferred_element_type=F32))
    yc = (jnp.concatenate(yc_groups, axis=1) * pscale_ref[...]).astype(BF16)
    m += z_ref[:, o3 + 2 * d:o3 + 3 * d].astype(F32) * jnp.dot(yc, wc_ref[...], preferred_element_type=F32)

    o_ref[...] = h_ref[...] + jnp.dot(m.astype(BF16), wout_ref[...], preferred_element_type=F32)


def _resident(a, layer, ndim):
    zeros = (0,) * ndim
    if a.ndim == ndim:
        return pl.BlockSpec(a.shape, lambda i: zeros, pipeline_mode=pl.Buffered(1))
    return pl.BlockSpec((None,) + a.shape[1:], lambda i: (layer,) + zeros, pipeline_mode=pl.Buffered(1))


def _mixer(z, h, seq, ln_g, ln_b, sgu_w, sgu_bias, conv_w, pool_w, pool_scale, w_a, w_b, w_c, w_out, layer,
           *, name):
    t, d = h.shape
    n_in = z.shape[-1]
    d_a, d_b, d_c = w_a.shape[-2], w_b.shape[-2], w_c.shape[-2]
    tm = _pick(MIXER_ROWS, seq)
    assert tm % SGU_BLOCK == 0 and tm >= POOL_HISTORY >= max(POOL_WINDOWS) - 1 and HALO >= CONV_W - 1
    params = (ln_g, ln_b, sgu_w, sgu_bias, conv_w, pool_w, pool_scale, w_a, w_b, w_c, w_out)
    ndims = (2, 2, 3, 2, 2, 3, 2, 2, 2, 2, 2)
    return pl.pallas_call(
        functools.partial(_mixer_kernel, tiles_per_seq=seq // tm, d_a=d_a, d_b=d_b, d_c=d_c),
        grid=(t // tm,),
        in_specs=[pl.BlockSpec((tm, n_in), lambda i: (i, 0)), pl.BlockSpec((tm, d), lambda i: (i, 0))]
                 + [_resident(a, layer, n) for a, n in zip(params, ndims)],
        out_specs=pl.BlockSpec((tm, d), lambda i: (i, 0)),
        out_shape=jax.ShapeDtypeStruct((t, d), F32),
        scratch_shapes=[pltpu.VMEM((HALO + tm, d_b), F32), pltpu.VMEM((POOL_HISTORY + tm, d_c), BF16),
                        pltpu.VMEM((len(POOL_WINDOWS), tm, POOL_HISTORY + tm), BF16)],
        compiler_params=_compiler_params("arbitrary"),
        name=name,
    )(z, h, *params)


def _ple_kernel(h_ref, p_ref, g_ref, wg_ref, wp_ref, fg_ref, o_ref, *, final_norm):
    h = h_ref[...]
    proj = jnp.dot(p_ref[...].astype(BF16), wp_ref[...], preferred_element_type=F32)
    gate = jnp.dot(_rmsnorm(h, g_ref[...]).astype(BF16), wg_ref[...], preferred_element_type=F32)
    out = h + _sigmoid(gate) * proj
    if final_norm:
        out = _rmsnorm(out, fg_ref[...])
    o_ref[...] = out


def _ple(h, p, norm_g, w_gate, w_proj, final_g, layer, final_norm, *, name):
    t, d = h.shape
    d_ple = p.shape[-1]
    tm = _pick(PLE_ROWS, t)
    tiles = t // tm
    return pl.pallas_call(
        functools.partial(_ple_kernel, final_norm=final_norm),
        grid=(tiles,),
        in_specs=[
            pl.BlockSpec((tm, d), lambda i: (i, 0)),
            pl.BlockSpec((tm, d_ple), lambda i: (layer * tiles + i, 0)),
            _resident(norm_g, layer, 2),
            _resident(w_gate, layer, 2),
            _resident(w_proj, layer, 2),
            pl.BlockSpec((1, d), lambda i: (0, 0)),
        ],
        out_specs=pl.BlockSpec((tm, d), lambda i: (i, 0)),
        out_shape=jax.ShapeDtypeStruct((t, d), F32),
        compiler_params=_compiler_params("parallel"),
        name=name,
    )(h, p, norm_g, w_gate, w_proj, final_g)


def kernel(x, p, ffn1_norm, ffn1_w_gate, ffn1_w_up, ffn1_w_down, mix_norm, w_in, sgu_norm_g, sgu_norm_b, sgu_w, sgu_b, conv_w, pool_w, pool_scale, w_branch_a, w_branch_b, w_branch_c, w_out, ffn2_norm, ffn2_w_gate, ffn2_w_up, ffn2_w_down, ple_norm, ple_w_gate, ple_w_proj, final_norm):
    bsz, seq, d = x.shape
    depth = p.shape[0]
    t = bsz * seq
    d_a = w_branch_a.shape[1]
    d_b = w_branch_b.shape[1]
    d_c = w_branch_c.shape[1]
    n_groups_a = sgu_w.shape[1]

    bf = lambda a: a.astype(BF16)
    row = lambda a: a[:, None, :]

    pos = jnp.arange(SGU_BLOCK)
    mask = (pos[None, :] // CHUNK) <= (pos[:, None] // CHUNK)
    sgu_w_masked = bf(jnp.where(mask[None, None], sgu_w, jnp.zeros_like(sgu_w)))
    sgu_bias = jnp.repeat(jnp.swapaxes(sgu_b, 1, 2), d_a // n_groups_a, axis=2)

    mixer_small = (row(sgu_norm_g), row(sgu_norm_b), sgu_w_masked, sgu_bias, conv_w, bf(pool_w), row(pool_scale))
    small_f32 = (w_branch_a, w_branch_b, w_branch_c, w_out, ple_w_gate, ple_w_proj)
    ple_g = row(ple_norm)
    final_g = final_norm[None, :]
    ffn1_f32 = (ffn1_w_gate, ffn1_w_up, ffn1_w_down)
    ffn2_f32 = (ffn2_w_gate, ffn2_w_up, ffn2_w_down)

    h = x.reshape(t, d)
    p2 = p.reshape(depth * t, p.shape[-1])
    w1 = tuple(bf(w[0]) for w in ffn1_f32)
    h, (w_in_l,) = _ffn(h, row(ffn1_norm), w1, 0, [(w_in, 0)], name="ffn1_L0")
    for layer in range(depth):
        more = layer + 1 < depth
        if layer:
            h, _ = _ffn(h, row(ffn1_norm), w1, layer, [], name=f"ffn1_L{layer}")
        z, cast = _inproj(h, row(mix_norm), w_in_l, layer, 2 * d_a, 3 * d_b + d_c,
                          [(w, layer) for w in ffn2_f32 + small_f32], name=f"inproj_L{layer}")
        w2, (w_a, w_b, w_c, w_o, ple_wg, ple_wp) = cast[:3], cast[3:]
        h = _mixer(z, h, seq, *mixer_small, w_a, w_b, w_c, w_o, layer, name=f"mixer_L{layer}")
        h, cast = _ffn(h, row(ffn2_norm), w2, layer, [(w, layer + 1) for w in ffn1_f32 + (w_in,)] if more else [],
                       name=f"ffn2_L{layer}")
        if more:
            w1, w_in_l = cast[:3], cast[3]
        h = _ple(h, p2, ple_g, ple_wg, ple_wp, final_g, layer, layer == depth - 1, name=f"ple_L{layer}")
    return h.reshape(bsz, seq, d)
```

```python
import functools

import jax
import jax.numpy as jnp
from jax import lax
from jax.experimental import pallas as pl
from jax.experimental.pallas import tpu as pltpu

EPS = 1e-6
CHUNK = 64
SGU_BLOCK = 128
CONV_W = 3
POOL_WINDOWS = (2, 4, 8, 16)
HALO = 16
POOL_HISTORY = 128

V7X_VMEM_BYTES = 64 * 1024 * 1024
VMEM_LIMIT_BYTES = V7X_VMEM_BYTES - 4 * 1024 * 1024
BF16_SUBLANES, LANES = 16, 128

FFN_ROWS, FFN_COLS = 1024, 512
INPROJ_ROWS, INPROJ_COLS = 1024, 2048
MIXER_ROWS = 256
PLE_ROWS = 512
NORM_ROWS = 256

F32 = jnp.float32
BF16 = jnp.bfloat16


def _pick(pref, n):
    t = min(pref, n)
    assert n % t == 0, (pref, n)
    return t


def _rmsnorm(x, g):
    return x * lax.rsqrt(jnp.mean(x * x, axis=-1, keepdims=True) + EPS) * g


def _gelu(x):
    return 0.5 * x * (1.0 + lax.erf(x * (2.0 ** -0.5)))


def _sigmoid(x):
    return 0.5 + 0.5 * jnp.tanh(0.5 * x)


def _for_rows(total, chunk, body):
    def step(r, carry):
        body(pl.ds(pl.multiple_of(r * chunk, chunk), chunk))
        return carry

    lax.fori_loop(0, total // chunk, step, 0)


def _compiler_params(*semantics):
    return pltpu.CompilerParams(dimension_semantics=semantics, vmem_limit_bytes=VMEM_LIMIT_BYTES)


def _ffn_kernel(*refs, n_ride):
    x_ref, g_ref, wg_ref, wu_ref, wd_ref = refs[:5]
    ride_in = refs[5:5 + n_ride]
    o_ref = refs[5 + n_ride]
    ride_out = refs[6 + n_ride:6 + 2 * n_ride]
    xn_ref = refs[-1]

    for src, dst in zip(ride_in, ride_out):
        dst[...] = src[...].astype(BF16)

    @pl.when(pl.program_id(1) == 0)
    def _():
        def init(rows):
            x = x_ref[rows, :]
            xn_ref[rows, :] = _rmsnorm(x, g_ref[...]).astype(BF16)
            o_ref[rows, :] = x
        _for_rows(x_ref.shape[0], NORM_ROWS, init)

    xn = xn_ref[...]
    gate = jnp.dot(xn, wg_ref[...], preferred_element_type=F32)
    up = jnp.dot(xn, wu_ref[...], preferred_element_type=F32)
    half = 0.5 * gate
    act = (0.5 * (half + half * jnp.tanh(half)) * up).astype(BF16)
    o_ref[...] += jnp.dot(act, wd_ref[...], preferred_element_type=F32)


def _ride_specs(shape, layer, n_i, n_j):
    r, c = shape
    best = None
    for rows_on_i in (True, False):
        n_r, n_c = (n_i, n_j) if rows_on_i else (n_j, n_i)
        for p_r in (n_r, n_r - 1, 1):
            for p_c in (n_c, n_c - 1, 1):
                if p_r < 1 or p_c < 1 or r % p_r or c % p_c or (r // p_r) % BF16_SUBLANES or (c // p_c) % LANES:
                    continue
                if (p_r not in (n_r, 1) and n_r != n_j) or (p_c not in (n_c, 1) and n_c != n_j):
                    continue
                size = (r // p_r) * (c // p_c)
                if best is None or size < best[0]:
                    best = (size, rows_on_i, p_r, p_c)
    assert best is not None, shape
    _, rows_on_i, p_r, p_c = best

    def index(i, j):
        gr, gc = (i, j) if rows_on_i else (j, i)
        return jnp.minimum(gr, p_r - 1), jnp.minimum(gc, p_c - 1)

    block = (r // p_r, c // p_c)
    return (pl.BlockSpec((None,) + block, lambda i, j: (layer,) + index(i, j)), pl.BlockSpec(block, index))


def _ffn(x, norm_g, weights, layer, ride, *, name):
    t, d = x.shape
    f = weights[-1].shape[0]
    tm = _pick(FFN_ROWS, t)
    tf = _pick(FFN_COLS, f)
    n_i, n_f = t // tm, f // tf
    ride_specs = [_ride_specs(a.shape[1:], l, n_i, n_f) for a, l in ride]
    out = pl.pallas_call(
        functools.partial(_ffn_kernel, n_ride=len(ride)),
        grid=(n_i, n_f),
        in_specs=[pl.BlockSpec((tm, d), lambda i, j: (i, 0)),
                  pl.BlockSpec((None, 1, d), lambda i, j: (layer, 0, 0)),
                  pl.BlockSpec((d, tf), lambda i, j: (0, j)),
                  pl.BlockSpec((d, tf), lambda i, j: (0, j)),
                  pl.BlockSpec((tf, d), lambda i, j: (j, 0))] + [s[0] for s in ride_specs],
        out_specs=[pl.BlockSpec((tm, d), lambda i, j: (i, 0))] + [s[1] for s in ride_specs],
        out_shape=[jax.ShapeDtypeStruct((t, d), F32)] + [jax.ShapeDtypeStruct(a.shape[1:], BF16) for a, _ in ride],
        scratch_shapes=[pltpu.VMEM((tm, d), BF16)],
        compiler_params=_compiler_params("arbitrary", "arbitrary"),
        name=name,
    )(x, norm_g, *weights, *(a for a, _ in ride))
    return out[0], tuple(out[1:])


def _inproj_kernel(*refs, tile_acts, n_ride):
    h_ref, g_ref, w_ref = refs[:3]
    ride_in = refs[3:3 + n_ride]
    z_ref = refs[3 + n_ride]
    ride_out = refs[4 + n_ride:4 + 2 * n_ride]
    xn_ref = refs[-1]
    j = pl.program_id(1)

    for src, dst in zip(ride_in, ride_out):
        dst[...] = src[...].astype(BF16)

    @pl.when(j == 0)
    def _():
        def init(rows):
            xn_ref[rows, :] = _rmsnorm(h_ref[rows, :], g_ref[...]).astype(BF16)
        _for_rows(h_ref.shape[0], NORM_ROWS, init)

    for lo, hi, act in tile_acts:
        @pl.when((j >= lo) & (j < hi))
        def _(act=act):
            z = jnp.dot(xn_ref[...], w_ref[...], preferred_element_type=F32)
            z_ref[...] = act(z).astype(z_ref.dtype)


def _inproj(h, norm_g, w_in, layer, n_gelu, n_linear, ride, *, name):
    t, d = h.shape
    n = w_in.shape[-1]
    tm = _pick(INPROJ_ROWS, t)
    tn = _pick(INPROJ_COLS, n)
    n_i, n_j = t // tm, n // tn
    assert n_gelu % tn == 0 and n_linear % tn == 0
    g_end, l_end = n_gelu // tn, (n_gelu + n_linear) // tn
    acts = [r for r in ((0, g_end, _gelu), (g_end, l_end, lambda z: z), (l_end, n_j, _sigmoid)) if r[0] < r[1]]
    ride_specs = [_ride_specs(a.shape[1:], l, n_i, n_j) for a, l in ride]
    out = pl.pallas_call(
        functools.partial(_inproj_kernel, tile_acts=tuple(acts), n_ride=len(ride)),
        grid=(n_i, n_j),
        in_specs=[
            pl.BlockSpec((tm, d), lambda i, j: (i, 0)),
            pl.BlockSpec((None, 1, d), lambda i, j: (layer, 0, 0)),
            pl.BlockSpec((d, tn), lambda i, j: (0, j)),
        ] + [s[0] for s in ride_specs],
        out_specs=[pl.BlockSpec((tm, tn), lambda i, j: (i, j))] + [s[1] for s in ride_specs],
        out_shape=[jax.ShapeDtypeStruct((t, n), BF16)] + [jax.ShapeDtypeStruct(a.shape[1:], BF16) for a, _ in ride],
        scratch_shapes=[pltpu.VMEM((tm, d), BF16)],
        compiler_params=_compiler_params("arbitrary", "arbitrary"),
        name=name,
    )(h, norm_g, w_in, *(a for a, _ in ride))
    return out[0], tuple(out[1:])


def _mixer_kernel(z_ref, h_ref, lng_ref, lnb_ref, sguw_ref, sgub_ref, convw_ref, poolw_ref, pscale_ref,
                  wa_ref, wb_ref, wc_ref, wout_ref, o_ref, ybuf_ref, xhist_ref, band_ref, bias_ref, *, tiles_per_seq,
                  d_a, d_b, d_c):
    tm, d = h_ref.shape
    n_groups_a = sguw_ref.shape[0]
    dg_a = d_a // n_groups_a
    dg_c = d_c // len(POOL_WINDOWS)
    o1 = 2 * d_a
    o2 = o1 + 3 * d_b
    o3 = o2 + d_c
    tile_in_seq = pl.program_id(0) % tiles_per_seq

    @pl.when(tile_in_seq == 0)
    def _():
        ybuf_ref[0:HALO, :] = jnp.zeros((HALO, d_b), F32)
        xhist_ref[0:POOL_HISTORY, :] = jnp.zeros((POOL_HISTORY, d_c), BF16)

    @pl.when(pl.program_id(0) == 0)
    def _():
        for g in range(n_groups_a):
            bias_ref[:, g * dg_a:(g + 1) * dg_a] = jnp.broadcast_to(sgub_ref[:, g:g + 1], (SGU_BLOCK, dg_a))
        t_idx = lax.broadcasted_iota(jnp.int32, (tm, POOL_HISTORY + tm), 0) + POOL_HISTORY
        lag = t_idx - lax.broadcasted_iota(jnp.int32, (tm, POOL_HISTORY + tm), 1)
        for g, w in enumerate(POOL_WINDOWS):
            band_ref[g] = jnp.where((lag >= 0) & (lag < w), 1.0, 0.0).astype(BF16)

    xhist_ref[POOL_HISTORY:POOL_HISTORY + tm, :] = z_ref[:, o2:o3]
    totals = [jnp.dot(band_ref[g], xhist_ref[:, g * dg_c:(g + 1) * dg_c], preferred_element_type=F32)
              for g in range(len(POOL_WINDOWS))]
    xhist_ref[0:POOL_HISTORY, :] = xhist_ref[tm:tm + POOL_HISTORY, :]

    v = z_ref[:, d_a:o1].astype(F32)
    mu = jnp.mean(v, axis=-1, keepdims=True)
    vc = v - mu
    var = jnp.mean(vc * vc, axis=-1, keepdims=True)
    vn = (vc * lax.rsqrt(var + EPS) * lng_ref[...] + lnb_ref[...]).astype(BF16)
    blocks = []
    for blk in range(tm // SGU_BLOCK):
        rows = slice(blk * SGU_BLOCK, (blk + 1) * SGU_BLOCK)
        groups = [jnp.dot(sguw_ref[g], vn[rows, g * dg_a:(g + 1) * dg_a], preferred_element_type=F32)
                  for g in range(n_groups_a)]
        blocks.append(jnp.concatenate(groups, axis=1) + bias_ref[...])
    vm = jnp.concatenate(blocks, axis=0) if len(blocks) > 1 else blocks[0]
    ya = (z_ref[:, 0:d_a].astype(F32) * vm).astype(BF16)
    m = z_ref[:, o3:o3 + d].astype(F32) * jnp.dot(ya, wa_ref[...], preferred_element_type=F32)

    ybuf_ref[HALO:HALO + tm, :] = (z_ref[:, o1 + d_b:o1 + 2 * d_b].astype(F32)
                                   * z_ref[:, o1 + 2 * d_b:o2].astype(F32))
    conv = convw_ref[CONV_W - 1:CONV_W, :] * ybuf_ref[HALO:HALO + tm, :]
    for k in range(CONV_W - 1):
        back = CONV_W - 1 - k
        conv += convw_ref[k:k + 1, :] * ybuf_ref[HALO - back:HALO - back + tm, :]
    yb = (z_ref[:, o1:o1 + d_b].astype(F32) * conv).astype(BF16)
    ybuf_ref[0:HALO, :] = ybuf_ref[tm:tm + HALO, :]
    m += z_ref[:, o3 + d:o3 + 2 * d].astype(F32) * jnp.dot(yb, wb_ref[...], preferred_element_type=F32)

    frames = (tile_in_seq * tm + 1 + lax.broadcasted_iota(jnp.int32, (tm, 1), 0)).astype(F32)
    yc_groups = []
    for g, w in enumerate(POOL_WINDOWS):
        cols = slice(g * dg_c, (g + 1) * dg_c)
        pooled = (totals[g] * (1.0 / jnp.minimum(frames, F32(w))) - z_ref[:, o2 + g * dg_c:o2 + (g + 1) * dg_c]
                  .astype(F32)).astype(BF16)
        yc_groups.append(jnp.dot(pooled, poolw_ref[g], preferred_element_type=F32))
    yc = (jnp.concatenate(yc_groups, axis=1) * pscale_ref[...]).astype(BF16)
    m += z_ref[:, o3 + 2 * d:o3 + 3 * d].astype(F32) * jnp.dot(yc, wc_ref[...], preferred_element_type=F32)

    o_ref[...] = h_ref[...] + jnp.dot(m.astype(BF16), wout_ref[...], preferred_element_type=F32)


def _resident(a, layer, ndim):
    zeros = (0,) * ndim
    if a.ndim == ndim:
        return pl.BlockSpec(a.shape, lambda i: zeros, pipeline_mode=pl.Buffered(1))
    return pl.BlockSpec((None,) + a.shape[1:], lambda i: (layer,) + zeros, pipeline_mode=pl.Buffered(1))


def _mixer(z, h, seq, ln_g, ln_b, sgu_w, sgu_bias, conv_w, pool_w, pool_scale, w_a, w_b, w_c, w_out, layer,
           *, name):
    t, d = h.shape
    n_in = z.shape[-1]
    d_a, d_b, d_c = w_a.shape[-2], w_b.shape[-2], w_c.shape[-2]
    tm = _pick(MIXER_ROWS, seq)
    assert tm % SGU_BLOCK == 0 and tm >= POOL_HISTORY >= max(POOL_WINDOWS) - 1 and HALO >= CONV_W - 1
    params = (ln_g, ln_b, sgu_w, sgu_bias, conv_w, pool_w, pool_scale, w_a, w_b, w_c, w_out)
    ndims = (2, 2, 3, 2, 2, 3, 2, 2, 2, 2, 2)
    return pl.pallas_call(
        functools.partial(_mixer_kernel, tiles_per_seq=seq // tm, d_a=d_a, d_b=d_b, d_c=d_c),
        grid=(t // tm,),
        in_specs=[pl.BlockSpec((tm, n_in), lambda i: (i, 0)), pl.BlockSpec((tm, d), lambda i: (i, 0))]
                 + [_resident(a, layer, n) for a, n in zip(params, ndims)],
        out_specs=pl.BlockSpec((tm, d), lambda i: (i, 0)),
        out_shape=jax.ShapeDtypeStruct((t, d), F32),
        scratch_shapes=[pltpu.VMEM((HALO + tm, d_b), F32), pltpu.VMEM((POOL_HISTORY + tm, d_c), BF16),
                        pltpu.VMEM((len(POOL_WINDOWS), tm, POOL_HISTORY + tm), BF16),
                        pltpu.VMEM((SGU_BLOCK, d_a), F32)],
        compiler_params=_compiler_params("arbitrary"),
        name=name,
    )(z, h, *params)


def _ple_kernel(h_ref, p_ref, g_ref, wg_ref, wp_ref, fg_ref, o_ref, *, final_norm):
    h = h_ref[...]
    proj = jnp.dot(p_ref[...].astype(BF16), wp_ref[...], preferred_element_type=F32)
    gate = jnp.dot(_rmsnorm(h, g_ref[...]).astype(BF16), wg_ref[...], preferred_element_type=F32)
    out = h + _sigmoid(gate) * proj
    if final_norm:
        out = _rmsnorm(out, fg_ref[...])
    o_ref[...] = out


def _ple(h, p, norm_g, w_gate, w_proj, final_g, layer, final_norm, *, name):
    t, d = h.shape
    d_ple = p.shape[-1]
    tm = _pick(PLE_ROWS, t)
    tiles = t // tm
    return pl.pallas_call(
        functools.partial(_ple_kernel, final_norm=final_norm),
        grid=(tiles,),
        in_specs=[
            pl.BlockSpec((tm, d), lambda i: (i, 0)),
            pl.BlockSpec((tm, d_ple), lambda i: (layer * tiles + i, 0)),
            _resident(norm_g, layer, 2),
            _resident(w_gate, layer, 2),
            _resident(w_proj, layer, 2),
            pl.BlockSpec((1, d), lambda i: (0, 0)),
        ],
        out_specs=pl.BlockSpec((tm, d), lambda i: (i, 0)),
        out_shape=jax.ShapeDtypeStruct((t, d), F32),
        compiler_params=_compiler_params("parallel"),
        name=name,
    )(h, p, norm_g, w_gate, w_proj, final_g)


def kernel(x, p, ffn1_norm, ffn1_w_gate, ffn1_w_up, ffn1_w_down, mix_norm, w_in, sgu_norm_g, sgu_norm_b, sgu_w, sgu_b, conv_w, pool_w, pool_scale, w_branch_a, w_branch_b, w_branch_c, w_out, ffn2_norm, ffn2_w_gate, ffn2_w_up, ffn2_w_down, ple_norm, ple_w_gate, ple_w_proj, final_norm):
    bsz, seq, d = x.shape
    depth = p.shape[0]
    t = bsz * seq
    d_a = w_branch_a.shape[1]
    d_b = w_branch_b.shape[1]
    d_c = w_branch_c.shape[1]

    bf = lambda a: a.astype(BF16)
    row = lambda a: a[:, None, :]

    pos = jnp.arange(SGU_BLOCK)
    mask = (pos[None, :] // CHUNK) <= (pos[:, None] // CHUNK)
    sgu_w_masked = bf(jnp.where(mask[None, None], sgu_w, jnp.zeros_like(sgu_w)))
    sgu_bias = jnp.swapaxes(sgu_b, 1, 2)

    mixer_small = (row(sgu_norm_g), row(sgu_norm_b), sgu_w_masked, sgu_bias, conv_w, bf(pool_w), row(pool_scale))
    small_f32 = (w_branch_a, w_branch_b, w_branch_c, w_out, ple_w_gate, ple_w_proj)
    ple_g = row(ple_norm)
    final_g = final_norm[None, :]
    ffn1_f32 = (ffn1_w_gate, ffn1_w_up, ffn1_w_down)
    ffn2_f32 = (ffn2_w_gate, ffn2_w_up, ffn2_w_down)

    h = x.reshape(t, d)
    p2 = p.reshape(depth * t, p.shape[-1])
    w1 = tuple(bf(w[0]) for w in ffn1_f32)
    h, (w_in_l,) = _ffn(h, row(ffn1_norm), w1, 0, [(w_in, 0)], name="ffn1_L0")
    for layer in range(depth):
        more = layer + 1 < depth
        if layer:
            h, _ = _ffn(h, row(ffn1_norm), w1, layer, [], name=f"ffn1_L{layer}")
        z, cast = _inproj(h, row(mix_norm), w_in_l, layer, 2 * d_a, 3 * d_b + d_c,
                          [(w, layer) for w in ffn2_f32 + small_f32], name=f"inproj_L{layer}")
        w2, (w_a, w_b, w_c, w_o, ple_wg, ple_wp) = cast[:3], cast[3:]
        h = _mixer(z, h, seq, *mixer_small, w_a, w_b, w_c, w_o, layer, name=f"mixer_L{layer}")
        h, cast = _ffn(h, row(ffn2_norm), w2, layer, [(w, layer + 1) for w in ffn1_f32 + (w_in,)] if more else [],
                       name=f"ffn2_L{layer}")
        if more:
            w1, w_in_l = cast[:3], cast[3]
        h = _ple(h, p2, ple_g, ple_wg, ple_wp, final_g, layer, layer == depth - 1, name=f"ple_L{layer}")
    return h.reshape(bsz, seq, d)
```
